```python
import math
import jax, jax.numpy as jnp
from jax import lax
import numpy as np

D_MODEL = 1024
BATCH = 2
SEQ = 16384
DEPTH = 2

N_A = DEPTH // 2
N_B = DEPTH - N_A

HEAD_DIM = 64
MIX_W = D_MODEL
MEM_HEADS = 4
MEM_W = MEM_HEADS * HEAD_DIM
REC_W = MIX_W - MEM_W
REC_BLOCKS = REC_W // HEAD_DIM
SB_W = MIX_W - MEM_W
SB_HEADS = SB_W // HEAD_DIM
N_MEM = 256
CONV_W = 4
LRU_C = 8.0
D_FF = 2816
Q_BLOCK = 128
EPS = 1e-6

kernel_name = "yoco_rglru_stickbreaking_hybrid"


def rms_norm(x, g):
    xf = x.astype(jnp.float32)
    y = xf * lax.rsqrt(jnp.mean(xf * xf, axis=-1, keepdims=True) + EPS)
    return (y * g.astype(jnp.float32)).astype(x.dtype)


def swiglu(h, w_gate, w_up, w_down):
    return (jax.nn.silu(h @ w_gate) * (h @ w_up)) @ w_down


def causal_depthwise_conv(x, w, b):
    c = x.shape[-1]
    y = lax.conv_general_dilated(
        x, w[:, None, :].astype(x.dtype), window_strides=(1,),
        padding=[(CONV_W - 1, 0)], dimension_numbers=("NWC", "WIO", "NWC"),
        feature_group_count=c)
    return y + b


def rg_lru(xc, gate_w, gate_b, lam):
    bsz, s, _ = xc.shape
    xf = xc.astype(jnp.float32)
    xb = xf.reshape(bsz, s, REC_BLOCKS, HEAD_DIM)
    gates = jnp.einsum("bsnc,gncd->gbsnd", xb, gate_w.astype(jnp.float32))
    gates = gates.reshape(2, bsz, s, REC_W) + gate_b.astype(jnp.float32)[:, None, None, :]
    r = jax.nn.sigmoid(gates[0])
    i = jax.nn.sigmoid(gates[1])
    log_a = -LRU_C * r * jax.nn.softplus(-lam.astype(jnp.float32))
    a = jnp.exp(log_a)
    b = jnp.sqrt(-jnp.expm1(2.0 * log_a)) * (i * xf)

    def combine(c1, c2):
        a1, b1 = c1
        a2, b2 = c2
        return a1 * a2, a2 * b1 + b2

    _, h = lax.associative_scan(combine, (a, b), axis=1)
    return h.astype(xc.dtype)


def stick_breaking_attention(q, k, v):
    dtype = q.dtype
    q = q.astype(jnp.float32) * (1.0 / math.sqrt(HEAD_DIM))
    k = k.astype(jnp.float32)
    v = v.astype(jnp.float32)
    bsz, nh, s, dh = q.shape
    n_blk = s // Q_BLOCK
    ar = jnp.arange(Q_BLOCK)

    def q_block(qb):
        q_blk = lax.dynamic_slice_in_dim(q, qb * Q_BLOCK, Q_BLOCK, axis=2)
        t_idx = qb * Q_BLOCK + ar

        def body(i, carry):
            acc, log_rest = carry
            kb = qb - i
            k_blk = lax.dynamic_slice_in_dim(k, kb * Q_BLOCK, Q_BLOCK, axis=2)
            v_blk = lax.dynamic_slice_in_dim(v, kb * Q_BLOCK, Q_BLOCK, axis=2)
            z = jnp.einsum("bhqd,bhkd->bhqk", q_blk, k_blk)
            causal = (kb * Q_BLOCK + ar)[None, :] < t_idx[:, None]
            log_1mb = jnp.where(causal, jax.nn.log_sigmoid(-z), 0.0)
            suffix = lax.cumsum(log_1mb, axis=3, reverse=True) - log_1mb
            log_w = jax.nn.log_sigmoid(z) + suffix + log_rest[..., None]
            w = jnp.where(causal, jnp.exp(log_w), 0.0)
            acc = acc + jnp.einsum("bhqk,bhkd->bhqd", w, v_blk)
            log_rest = log_rest + jnp.sum(log_1mb, axis=3)
            return acc, log_rest

        init = (jnp.zeros((bsz, nh, Q_BLOCK, dh), jnp.float32),
                jnp.zeros((bsz, nh, Q_BLOCK), jnp.float32))
        acc, _ = lax.fori_loop(0, qb + 1, body, init)
        return acc

    out = lax.map(q_block, jnp.arange(n_blk))
    out = out.transpose(1, 0, 3, 2, 4).reshape(bsz, s, nh * dh)
    return out.astype(dtype)


def memory_attention(q_mem, mem_n, w_mem_kv):
    bsz, s, _ = q_mem.shape
    q = q_mem.reshape(bsz, s, MEM_HEADS, HEAD_DIM).astype(jnp.float32)
    kv = (mem_n @ w_mem_kv).reshape(bsz, N_MEM, 2, MEM_HEADS, HEAD_DIM).astype(jnp.float32)
    scores = jnp.einsum("bshd,bmhd->bhsm", q, kv[:, :, 0]) * (1.0 / math.sqrt(HEAD_DIM))
    p = jax.nn.softmax(scores, axis=-1)
    out = jnp.einsum("bhsm,bmhd->bshd", p, kv[:, :, 1])
    return out.reshape(bsz, s, MEM_W).astype(q_mem.dtype)


def setup_inputs(seed: int = 0) -> dict:
    key = jax.random.key(seed)
    ks = jax.random.split(key, 20)
    nrm = jax.random.normal
    d = D_MODEL
    x = nrm(ks[0], (BATCH, SEQ, d), jnp.float32)
    mem = nrm(ks[1], (BATCH, N_MEM, d), jnp.float32)
    ffn_w_gate = nrm(ks[2], (DEPTH, 2, d, D_FF), jnp.float32) * d ** -0.5
    ffn_w_up = nrm(ks[3], (DEPTH, 2, d, D_FF), jnp.float32) * d ** -0.5
    ffn_w_down = nrm(ks[4], (DEPTH, 2, D_FF, d), jnp.float32) * D_FF ** -0.5
    norm_g = 1.0 + 0.1 * nrm(ks[5], (DEPTH, 6, d), jnp.float32)
    mem_norm_g = 1.0 + 0.1 * nrm(ks[6], (DEPTH, d), jnp.float32)
    w_mem_kv = nrm(ks[7], (DEPTH, d, 2 * MEM_W), jnp.float32) * d ** -0.5
    w_mix_out = nrm(ks[8], (DEPTH, MIX_W, d), jnp.float32) * MIX_W ** -0.5
    a_w_in = nrm(ks[9], (N_A, d, 2 * REC_W + MEM_W), jnp.float32) * d ** -0.5
    a_conv_w = nrm(ks[10], (N_A, CONV_W, REC_W), jnp.float32) * CONV_W ** -0.5
    a_conv_b = 0.01 * nrm(ks[11], (N_A, REC_W), jnp.float32)
    a_gate_w = nrm(ks[12], (N_A, 2, REC_BLOCKS, HEAD_DIM, HEAD_DIM), jnp.float32) * HEAD_DIM ** -0.5
    a_gate_b = 0.01 * nrm(ks[13], (N_A, 2, REC_W), jnp.float32)
    u = jax.random.uniform(ks[14], (N_A, REC_W), jnp.float32, minval=0.9, maxval=0.999)
    p = u ** (1.0 / LRU_C)
    a_lambda = jnp.log(p) - jnp.log1p(-p)
    b_w_in = nrm(ks[15], (N_B, d, SB_W + MEM_W), jnp.float32) * d ** -0.5
    kv_norm_g = 1.0 + 0.1 * nrm(ks[16], (d,), jnp.float32)
    w_kv_shared = nrm(ks[17], (d, 2 * SB_W), jnp.float32) * d ** -0.5
    return {"x": x, "mem": mem, "ffn_w_gate": ffn_w_gate, "ffn_w_up": ffn_w_up,
            "ffn_w_down": ffn_w_down, "norm_g": norm_g, "mem_norm_g": mem_norm_g,
            "w_mem_kv": w_mem_kv, "w_mix_out": w_mix_out, "a_w_in": a_w_in,
            "a_conv_w": a_conv_w, "a_conv_b": a_conv_b, "a_gate_w": a_gate_w,
            "a_gate_b": a_gate_b, "a_lambda": a_lambda, "b_w_in": b_w_in,
            "kv_norm_g": kv_norm_g, "w_kv_shared": w_kv_shared}


def reference(x, mem, ffn_w_gate, ffn_w_up, ffn_w_down, norm_g, mem_norm_g, w_mem_kv,
              w_mix_out, a_w_in, a_conv_w, a_conv_b, a_gate_w, a_gate_b, a_lambda,
              b_w_in, kv_norm_g, w_kv_shared):
    bsz, s, _ = x.shape
    k_sh = v_sh = None
    for layer in range(DEPTH):
        g = norm_g[layer]
        if layer == N_A:
            kv = (rms_norm(x, kv_norm_g) @ w_kv_shared).reshape(bsz, s, 2, SB_HEADS, HEAD_DIM)
            k_sh = kv[:, :, 0].transpose(0, 2, 1, 3)
            v_sh = kv[:, :, 1].transpose(0, 2, 1, 3)
        f = swiglu(rms_norm(x, g[0]), ffn_w_gate[layer, 0], ffn_w_up[layer, 0], ffn_w_down[layer, 0])
        x = x + 0.5 * rms_norm(f, g[1])
        hn = rms_norm(x, g[2])
        mem_n = rms_norm(mem, mem_norm_g[layer])
        if layer < N_A:
            proj = hn @ a_w_in[layer]
            x_rec = proj[..., :REC_W]
            x_gate = proj[..., REC_W:2 * REC_W]
            q_mem = proj[..., 2 * REC_W:]
            xc = causal_depthwise_conv(x_rec, a_conv_w[layer], a_conv_b[layer])
            h_rec = rg_lru(xc, a_gate_w[layer], a_gate_b[layer], a_lambda[layer])
            y_main = jax.nn.gelu(x_gate) * h_rec
        else:
            proj = hn @ b_w_in[layer - N_A]
            q_sb = proj[..., :SB_W].reshape(bsz, s, SB_HEADS, HEAD_DIM).transpose(0, 2, 1, 3)
            q_mem = proj[..., SB_W:]
            y_main = stick_breaking_attention(q_sb, k_sh, v_sh)
        y_mem = memory_attention(q_mem, mem_n, w_mem_kv[layer])
        mix = jnp.concatenate([y_main, y_mem], axis=-1) @ w_mix_out[layer]
        x = x + rms_norm(mix, g[3])
        f = swiglu(rms_norm(x, g[4]), ffn_w_gate[layer, 1], ffn_w_up[layer, 1], ffn_w_down[layer, 1])
        x = x + 0.5 * rms_norm(f, g[5])
    return x
```

```python
import functools
import math

import jax
import jax.numpy as jnp
from jax import lax
from jax.experimental import pallas as pl
from jax.experimental.pallas import tpu as pltpu

D_MODEL = 1024
HEAD_DIM = 64
MEM_HEADS = 4
MEM_W = MEM_HEADS * HEAD_DIM
REC_W = D_MODEL - MEM_W
SB_W = D_MODEL - MEM_W
N_MEM = 256
CONV_W = 4
LRU_C = 8.0
D_FF = 2816
EPS = 1e-6

LANES = 128
SUBLANES = 8
MXU_DIM = 256
VMEM_LIMIT_BYTES = 56 * 1024 * 1024

ROW_TILE = 512
FF_CHUNK = MXU_DIM
N_FF_CHUNKS = D_FF // FF_CHUNK
SEQ_TILE = 512
ATT_TILE = 256
HEAD_PAIR = 2 * HEAD_DIM

F32_EXP_ZERO_BELOW = -110.0

_F32 = jnp.float32
_BF16 = jnp.bfloat16


def _params(semantics):
    return pltpu.CompilerParams(dimension_semantics=semantics, vmem_limit_bytes=VMEM_LIMIT_BYTES)


def _rms(x, g):
    ms = jnp.mean(x * x, axis=-1, keepdims=True)
    return x * lax.rsqrt(ms + EPS) * g


def _softplus(z):
    return jnp.maximum(z, 0.0) + jnp.log1p(jnp.exp(-jnp.abs(z)))


def _dot(a, b):
    return jnp.dot(a, b, preferred_element_type=_F32)


def _ffn_body(x_ref, gin_ref, gout_ref, wgu_ref, wd_ref, o_ref, xn_ref, acc_ref):
    xn_ref[...] = _rms(x_ref[...], gin_ref[...]).astype(_BF16)
    acc_ref[...] = jnp.zeros_like(acc_ref)

    def chunk(c, carry):
        gu = _dot(xn_ref[...], wgu_ref[c])
        gate = gu[:, :FF_CHUNK]
        up = gu[:, FF_CHUNK:]
        act = (gate * jax.nn.sigmoid(gate) * up).astype(_BF16)
        acc_ref[...] += _dot(act, wd_ref[c])
        return carry

    lax.fori_loop(0, N_FF_CHUNKS, chunk, 0)
    o_ref[...] = x_ref[...] + 0.5 * _rms(acc_ref[...], gout_ref[...])


def _ffn(x2d, g_in, g_out, wgu, wd):
    rows = x2d.shape[0]
    return pl.pallas_call(
        _ffn_body,
        out_shape=jax.ShapeDtypeStruct((rows, D_MODEL), _F32),
        grid=(rows // ROW_TILE,),
        in_specs=[
            pl.BlockSpec((ROW_TILE, D_MODEL), lambda i: (i, 0)),
            pl.BlockSpec((1, D_MODEL), lambda i: (0, 0)),
            pl.BlockSpec((1, D_MODEL), lambda i: (0, 0)),
            pl.BlockSpec((N_FF_CHUNKS, D_MODEL, 2 * FF_CHUNK), lambda i: (0, 0, 0)),
            pl.BlockSpec((N_FF_CHUNKS, FF_CHUNK, D_MODEL), lambda i: (0, 0, 0)),
        ],
        out_specs=pl.BlockSpec((ROW_TILE, D_MODEL), lambda i: (i, 0)),
        scratch_shapes=[pltpu.VMEM((ROW_TILE, D_MODEL), _BF16),
                        pltpu.VMEM((ROW_TILE, D_MODEL), _F32)],
        compiler_params=_params(("arbitrary",)),
        name="ffn",
    )(x2d, g_in, g_out, wgu, wd)


def _proj_body(x_ref, g_ref, w_ref, o_ref):
    xn = _rms(x_ref[...], g_ref[...]).astype(_BF16)
    o_ref[...] = _dot(xn, w_ref[...]).astype(o_ref.dtype)


def _proj(x2d, g, w, out_dtype):
    rows = x2d.shape[0]
    n_out = w.shape[1]
    return pl.pallas_call(
        _proj_body,
        out_shape=jax.ShapeDtypeStruct((rows, n_out), out_dtype),
        grid=(rows // ROW_TILE,),
        in_specs=[
            pl.BlockSpec((ROW_TILE, D_MODEL), lambda i: (i, 0)),
            pl.BlockSpec((1, D_MODEL), lambda i: (0, 0)),
            pl.BlockSpec((D_MODEL, n_out), lambda i: (0, 0)),
        ],
        out_specs=pl.BlockSpec((ROW_TILE, n_out), lambda i: (i, 0)),
        compiler_params=_params(("arbitrary",)),
        name="proj",
    )(x2d, g, w)


def _memkv_body(mem_ref, g_ref, w_ref, o_ref):
    mn = _rms(mem_ref[0], g_ref[...]).astype(_BF16)
    o_ref[0] = _dot(mn, w_ref[...])


def _memkv(mem, g, w):
    bsz = mem.shape[0]
    return pl.pallas_call(
        _memkv_body,
        out_shape=jax.ShapeDtypeStruct((bsz, N_MEM, 2 * MEM_W), _F32),
        grid=(bsz,),
        in_specs=[
            pl.BlockSpec((1, N_MEM, D_MODEL), lambda b: (b, 0, 0)),
            pl.BlockSpec((1, D_MODEL), lambda b: (0, 0)),
            pl.BlockSpec((D_MODEL, 2 * MEM_W), lambda b: (0, 0)),
        ],
        out_specs=pl.BlockSpec((1, N_MEM, 2 * MEM_W), lambda b: (b, 0, 0)),
        compiler_params=_params(("arbitrary",)),
        name="memkv",
    )(mem, g, w)


def _mem_block_diag(kvm):
    bsz = kvm.shape[0]
    k = kvm[:, :, :MEM_W].reshape(bsz, N_MEM, MEM_HEADS, HEAD_DIM)
    v = kvm[:, :, MEM_W:].reshape(bsz, N_MEM, MEM_HEADS, HEAD_DIM)
    eye = jnp.eye(MEM_HEADS, dtype=_F32)
    kbd = jnp.einsum("bmhd,hg->bhdgm", k * (1.0 / math.sqrt(HEAD_DIM)), eye)
    kbd = kbd.reshape(bsz, MEM_W, MEM_HEADS * N_MEM)
    vbd = jnp.einsum("bmhd,hg->bgmhd", v, eye).reshape(bsz, MEM_HEADS * N_MEM, MEM_W)
    return kbd.astype(_BF16), vbd.astype(_BF16)


def _rglru_body(xrec_ref, xgate_ref, cw_ref, cb_ref, wg_ref, gb_ref, lam_ref, o_ref,
                xbuf_ref, h_ref):
    s = pl.program_id(1)

    @pl.when(s == 0)
    def _():
        xbuf_ref[0:SUBLANES, :] = jnp.zeros((SUBLANES, REC_W), _F32)
        h_ref[...] = jnp.zeros_like(h_ref)

    xbuf_ref[SUBLANES:, :] = xrec_ref[0]
    xc = cb_ref[...] + cw_ref[CONV_W - 1:CONV_W, :] * xbuf_ref[SUBLANES:, :]
    for k in range(CONV_W - 1):
        off = SUBLANES - (CONV_W - 1) + k
        xc = xc + cw_ref[k:k + 1, :] * xbuf_ref[off:off + SEQ_TILE, :]
    xbuf_ref[0:SUBLANES, :] = xbuf_ref[SEQ_TILE:SEQ_TILE + SUBLANES, :]

    gates = _dot(xc.astype(_BF16), wg_ref[...]) + gb_ref[...]
    r = jax.nn.sigmoid(gates[:, :REC_W])
    i = jax.nn.sigmoid(gates[:, REC_W:])
    log_a = (-LRU_C) * r * _softplus(-lam_ref[...])
    a = jnp.exp(log_a)
    b = jnp.sqrt(-jnp.tanh(log_a) * (1.0 + a * a)) * (i * xc)

    n_groups = SEQ_TILE // SUBLANES
    a3 = a.reshape(n_groups, SUBLANES, REC_W)
    b3 = b.reshape(n_groups, SUBLANES, REC_W)
    sub = lax.broadcasted_iota(jnp.int32, (n_groups, SUBLANES, REC_W), 1)
    d = 1
    while d < SUBLANES:
        keep = sub >= d
        a_prev = jnp.where(keep, pltpu.roll(a3, d, 1), 1.0)
        b_prev = jnp.where(keep, pltpu.roll(b3, d, 1), 0.0)
        b3 = a3 * b_prev + b3
        a3 = a3 * a_prev
        d *= 2
    h = h_ref[...]
    outs = []
    for g in range(n_groups):
        hg = a3[g] * h + b3[g]
        outs.append(hg)
        h = hg[SUBLANES - 1:SUBLANES, :]
    h_ref[...] = h
    h_all = jnp.concatenate(outs, axis=0)
    o_ref[0] = (jax.nn.gelu(xgate_ref[0], approximate=True) * h_all).astype(o_ref.dtype)


def _rglru(proj, conv_w, conv_b, wgate, gate_b, lam):
    bsz, seq, _ = proj.shape
    return pl.pallas_call(
        _rglru_body,
        out_shape=jax.ShapeDtypeStruct((bsz, seq, REC_W), _BF16),
        grid=(bsz, seq // SEQ_TILE),
        in_specs=[
            pl.BlockSpec((1, SEQ_TILE, REC_W), lambda b, s: (b, s, 0)),
            pl.BlockSpec((1, SEQ_TILE, REC_W), lambda b, s: (b, s, 1)),
            pl.BlockSpec((CONV_W, REC_W), lambda b, s: (0, 0)),
            pl.BlockSpec((1, REC_W), lambda b, s: (0, 0)),
            pl.BlockSpec((REC_W, 2 * REC_W), lambda b, s: (0, 0)),
            pl.BlockSpec((1, 2 * REC_W), lambda b, s: (0, 0)),
            pl.BlockSpec((1, REC_W), lambda b, s: (0, 0)),
        ],
        out_specs=pl.BlockSpec((1, SEQ_TILE, REC_W), lambda b, s: (b, s, 0)),
        scratch_shapes=[pltpu.VMEM((SEQ_TILE + SUBLANES, REC_W), _F32),
                        pltpu.VMEM((1, REC_W), _F32)],
        compiler_params=_params(("arbitrary", "arbitrary")),
        name="rglru",
    )(proj, proj, conv_w, conv_b, wgate, gate_b, lam)


def _gate_block_diag(gate_w):
    n_blocks = gate_w.shape[1]
    eye = jnp.eye(n_blocks, dtype=gate_w.dtype)
    bd = jnp.einsum("gncd,nm->gncmd", gate_w, eye).reshape(2, REC_W, REC_W)
    return jnp.concatenate([bd[0], bd[1]], axis=1).astype(_BF16)


def _sb_body(q_ref, k_ref, v_ref, o_ref):
    qi = pl.program_id(2)
    q = q_ref[0]
    lane = lax.broadcasted_iota(jnp.int32, (ATT_TILE, HEAD_PAIR), 1)
    row = lax.broadcasted_iota(jnp.int32, (ATT_TILE, ATT_TILE), 0)
    col = lax.broadcasted_iota(jnp.int32, (ATT_TILE, ATT_TILE), 1)
    causal = col < row
    suffix_ones = (row >= col).astype(_BF16)
    scale = 1.0 / math.sqrt(HEAD_DIM)

    def tile(qh, j, rest, acc, masked):
        kb = k_ref[0, pl.ds(j * ATT_TILE, ATT_TILE), :]
        vb = v_ref[0, pl.ds(j * ATT_TILE, ATT_TILE), :]
        z = lax.dot_general(qh, kb, (((1,), (1,)), ((), ())), preferred_element_type=_F32)
        soft = _softplus(z)
        if masked:
            soft = jnp.where(causal, soft, 0.0)
        csum = _dot(soft.astype(_BF16), suffix_ones)
        w = jnp.exp(z - csum - rest)
        if masked:
            w = jnp.where(causal, w, 0.0)
        acc = acc + _dot(w.astype(_BF16), vb)
        rest = rest + jnp.sum(soft, axis=1, keepdims=True)
        return rest, acc

    accs = []
    for h in range(2):
        in_head = (lane >= h * HEAD_DIM) & (lane < (h + 1) * HEAD_DIM)
        qh = jnp.where(in_head, q, jnp.zeros_like(q)) * jnp.asarray(scale, q.dtype)
        rest0 = jnp.zeros((ATT_TILE, 1), _F32)
        acc0 = jnp.zeros((ATT_TILE, HEAD_PAIR), _F32)
        rest1, acc1 = tile(qh, qi, rest0, acc0, True)

        def cond(c):
            j, rest, _ = c
            return jnp.logical_and(j >= 0, jnp.min(rest) < -F32_EXP_ZERO_BELOW)

        def body(c, qh=qh):
            j, rest, acc = c
            rest, acc = tile(qh, j, rest, acc, False)
            return j - 1, rest, acc

        _, _, acc_h = lax.while_loop(cond, body, (qi - 1, rest1, acc1))
        accs.append(acc_h)
    o_ref[0] = jnp.where(lane < HEAD_DIM, accs[0], accs[1]).astype(o_ref.dtype)


def _sb_attention(proj, kv):
    bsz, seq, _ = proj.shape
    n_pairs = SB_W // HEAD_PAIR
    return pl.pallas_call(
        _sb_body,
        out_shape=jax.ShapeDtypeStruct((bsz, seq, SB_W), _BF16),
        grid=(bsz, n_pairs, seq // ATT_TILE),
        in_specs=[
            pl.BlockSpec((1, ATT_TILE, HEAD_PAIR), lambda b, p, i: (b, i, p)),
            pl.BlockSpec((1, seq, HEAD_PAIR), lambda b, p, i: (b, 0, p)),
            pl.BlockSpec((1, seq, HEAD_PAIR), lambda b, p, i: (b, 0, n_pairs + p)),
        ],
        out_specs=pl.BlockSpec((1, ATT_TILE, HEAD_PAIR), lambda b, p, i: (b, i, p)),
        compiler_params=_params(("arbitrary", "arbitrary", "arbitrary")),
        name="sb_attention",
    )(proj, kv, kv)


def _mix_body(ymain_ref, qmem_ref, kbd_ref, vbd_ref, wout_ref, x_ref, g_ref, o_ref):
    qm = qmem_ref[0].astype(_BF16)
    scores = _dot(qm, kbd_ref[0])
    probs = []
    for h in range(MEM_HEADS):
        sh = scores[:, h * N_MEM:(h + 1) * N_MEM]
        e = jnp.exp(sh - jnp.max(sh, axis=-1, keepdims=True))
        probs.append(e / jnp.sum(e, axis=-1, keepdims=True))
    p = jnp.concatenate(probs, axis=-1).astype(_BF16)
    ymem = _dot(p, vbd_ref[0]).astype(_BF16)
    mix = _dot(ymain_ref[0], wout_ref[0:SB_W, :]) + _dot(ymem, wout_ref[SB_W:, :])
    o_ref[0] = x_ref[0] + _rms(mix, g_ref[...])


def _mix_out(ymain, proj, qmem_block, kbd, vbd, wout, x, g):
    bsz, seq, _ = x.shape
    return pl.pallas_call(
        _mix_body,
        out_shape=jax.ShapeDtypeStruct((bsz, seq, D_MODEL), _F32),
        grid=(bsz, seq // ROW_TILE),
        in_specs=[
            pl.BlockSpec((1, ROW_TILE, SB_W), lambda b, s: (b, s, 0)),
            pl.BlockSpec((1, ROW_TILE, MEM_W), lambda b, s: (b, s, qmem_block)),
            pl.BlockSpec((1, MEM_W, MEM_HEADS * N_MEM), lambda b, s: (b, 0, 0)),
            pl.BlockSpec((1, MEM_HEADS * N_MEM, MEM_W), lambda b, s: (b, 0, 0)),
            pl.BlockSpec((D_MODEL, D_MODEL), lambda b, s: (0, 0)),
            pl.BlockSpec((1, ROW_TILE, D_MODEL), lambda b, s: (b, s, 0)),
            pl.BlockSpec((1, D_MODEL), lambda b, s: (0, 0)),
        ],
        out_specs=pl.BlockSpec((1, ROW_TILE, D_MODEL), lambda b, s: (b, s, 0)),
        compiler_params=_params(("arbitrary", "arbitrary")),
        name="mix_out",
    )(ymain, proj, kbd, vbd, wout, x, g)


def _ffn_weights(w_gate, w_up, w_down):
    wg = w_gate.reshape(D_MODEL, N_FF_CHUNKS, FF_CHUNK)
    wu = w_up.reshape(D_MODEL, N_FF_CHUNKS, FF_CHUNK)
    wgu = jnp.concatenate([wg, wu], axis=2).transpose(1, 0, 2).astype(_BF16)
    wd = w_down.reshape(N_FF_CHUNKS, FF_CHUNK, D_MODEL).astype(_BF16)
    return wgu, wd


def kernel(x, mem, ffn_w_gate, ffn_w_up, ffn_w_down, norm_g, mem_norm_g, w_mem_kv, w_mix_out,
           a_w_in, a_conv_w, a_conv_b, a_gate_w, a_gate_b, a_lambda, b_w_in, kv_norm_g,
           w_kv_shared):
    bsz, seq, _ = x.shape
    rows = bsz * seq
    row1 = lambda v: v.reshape(1, -1).astype(_F32)

    def ffn(xin, layer, half):
        wgu, wd = _ffn_weights(ffn_w_gate[layer, half], ffn_w_up[layer, half], ffn_w_down[layer, half])
        g_in = row1(norm_g[layer, 4 * half])
        g_out = row1(norm_g[layer, 4 * half + 1])
        return _ffn(xin.reshape(rows, D_MODEL), g_in, g_out, wgu, wd).reshape(bsz, seq, D_MODEL)

    def mem_kv(layer):
        kvm = _memkv(mem, row1(mem_norm_g[layer]), w_mem_kv[layer].astype(_BF16))
        return _mem_block_diag(kvm)

    x = ffn(x, 0, 0)
    proj = _proj(x.reshape(rows, D_MODEL), row1(norm_g[0, 2]), a_w_in[0].astype(_BF16), _F32)
    proj = proj.reshape(bsz, seq, -1)
    y_main = _rglru(proj, a_conv_w[0], row1(a_conv_b[0]), _gate_block_diag(a_gate_w[0]),
                    row1(a_gate_b[0]), row1(a_lambda[0]))
    kbd, vbd = mem_kv(0)
    x = _mix_out(y_main, proj, (2 * REC_W) // MEM_W, kbd, vbd, w_mix_out[0].astype(_BF16), x,
                 row1(norm_g[0, 3]))
    x = ffn(x, 0, 1)

    kv = _proj(x.reshape(rows, D_MODEL), row1(kv_norm_g), w_kv_shared.astype(_BF16), _BF16)
    kv = kv.reshape(bsz, seq, -1)
    x = ffn(x, 1, 0)
    proj = _proj(x.reshape(rows, D_MODEL), row1(norm_g[1, 2]), b_w_in[0].astype(_BF16), _BF16)
    proj = proj.reshape(bsz, seq, -1)
    y_main = _sb_attention(proj, kv)
    kbd, vbd = mem_kv(1)
    x = _mix_out(y_main, proj, SB_W // MEM_W, kbd, vbd, w_mix_out[1].astype(_BF16), x,
                 row1(norm_g[1, 3]))
    x = ffn(x, 1, 1)
    return x
```

```python
import functools
import math

import jax
import jax.numpy as jnp
from jax import lax
from jax.experimental import pallas as pl
from jax.experimental.pallas import tpu as pltpu

D_MODEL = 1024
HEAD_DIM = 64
MEM_HEADS = 4
MEM_W = MEM_HEADS * HEAD_DIM
REC_W = D_MODEL - MEM_W
SB_W = D_MODEL - MEM_W
N_MEM = 256
CONV_W = 4
LRU_C = 8.0
D_FF = 2816
EPS = 1e-6

SUBLANES = 8
MXU_DIM = 256
VMEM_LIMIT_BYTES = 56 * 1024 * 1024

ROW_TILE = 512
FF_CHUNK = MXU_DIM
N_FF_CHUNKS = D_FF // FF_CHUNK
SEQ_TILE = 512
ATT_TILE = 256
ATT_QBLOCK = 1024
ATT_SUB = ATT_QBLOCK // ATT_TILE
HEAD_PAIR = 2 * HEAD_DIM
MASKED_SCORE = -1e30
LOG2_E = math.log2(math.e)

F32_EXP_ZERO_BELOW = -110.0

_F32 = jnp.float32
_BF16 = jnp.bfloat16


def _params(semantics):
    return pltpu.CompilerParams(dimension_semantics=semantics, vmem_limit_bytes=VMEM_LIMIT_BYTES)


def _resident(shape):
    zeros = (0,) * len(shape)
    return pl.BlockSpec(shape, lambda *_: zeros, pipeline_mode=pl.Buffered(1))


def _rms(x, g):
    ms = jnp.mean(x * x, axis=-1, keepdims=True)
    return x * lax.rsqrt(ms + EPS) * g


def _softplus(z):
    return jnp.maximum(z, 0.0) + jnp.log(1.0 + jnp.exp(-jnp.abs(z)))


def _dot(a, b):
    return jnp.dot(a, b, preferred_element_type=_F32)


def _mem_attention(qm, kbd, vbd):
    scores = _dot(qm, kbd)
    probs = []
    for h in range(MEM_HEADS):
        sh = scores[:, h * N_MEM:(h + 1) * N_MEM]
        e = jnp.exp(sh - jnp.max(sh, axis=-1, keepdims=True))
        probs.append(e / jnp.sum(e, axis=-1, keepdims=True))
    p = jnp.concatenate(probs, axis=-1).astype(_BF16)
    return _dot(p, vbd)


def _block_body(*refs, has_mix, has_proj):
    refs = list(refs)
    x_ref = refs.pop(0)
    if has_mix:
        ymain_ref, qmem_ref, kbd_ref, vbd_ref, wout_ref, gmix_ref = refs[:6]
        del refs[:6]
    gin_ref, gout_ref, wg_ref, wu_ref, wd_ref = refs[:5]
    del refs[:5]
    if has_proj:
        gp_ref, wp_ref = refs[:2]
        del refs[:2]
    o_ref = refs.pop(0)
    if has_proj:
        p_ref = refs.pop(0)
    xres_ref, xn_ref, acc_ref = refs

    x = x_ref[0]
    if has_mix:
        ymem = _mem_attention(qmem_ref[0].astype(_BF16), kbd_ref[0], vbd_ref[0]).astype(_BF16)
        mix = _dot(ymain_ref[0], wout_ref[0:SB_W, :]) + _dot(ymem, wout_ref[SB_W:, :])
        x = x + _rms(mix, gmix_ref[...])
    xres_ref[...] = x
    xn_ref[...] = _rms(x, gin_ref[...]).astype(_BF16)

    for c in range(N_FF_CHUNKS):
        cols = slice(c * FF_CHUNK, (c + 1) * FF_CHUNK)
        gate = _dot(xn_ref[...], wg_ref[:, cols])
        up = _dot(xn_ref[...], wu_ref[:, cols])
        act = (gate * jax.nn.sigmoid(gate) * up).astype(_BF16)
        down = _dot(act, wd_ref[cols, :])
        if c == 0:
            acc_ref[...] = down
        else:
            acc_ref[...] += down

    xo = xres_ref[...] + 0.5 * _rms(acc_ref[...], gout_ref[...])
    o_ref[0] = xo
    if has_proj:
        p_ref[0] = _dot(_rms(xo, gp_ref[...]).astype(_BF16), wp_ref[...]).astype(p_ref.dtype)


def _block(x, ffn_w, g_in, g_out, mix=None, proj=None):
    bsz, seq, _ = x.shape
    row_spec = lambda width, col=0: pl.BlockSpec((1, ROW_TILE, width), lambda b, s: (b, s, col))
    per_batch = lambda a: pl.BlockSpec((1,) + a.shape[1:], lambda b, s: (b, 0, 0))
    gain = _resident((1, D_MODEL))

    args, specs = [x], [row_spec(D_MODEL)]
    if mix is not None:
        y_main, q_src, q_block, kbd, vbd, w_out, g_mix = mix
        args += [y_main, q_src, kbd, vbd, w_out, g_mix]
        specs += [row_spec(SB_W), row_spec(MEM_W, q_block), per_batch(kbd), per_batch(vbd),
                  _resident(w_out.shape), gain]
    args += [g_in, g_out, *ffn_w]
    specs += [gain, gain] + [_resident(w.shape) for w in ffn_w]
    out_shape = [jax.ShapeDtypeStruct((bsz, seq, D_MODEL), _F32)]
    out_specs = [row_spec(D_MODEL)]
    if proj is not None:
        g_proj, w_proj, p_dtype = proj
        args += [g_proj, w_proj]
        specs += [gain, _resident(w_proj.shape)]
        out_shape.append(jax.ShapeDtypeStruct((bsz, seq, w_proj.shape[1]), p_dtype))
        out_specs.append(row_spec(w_proj.shape[1]))

    outs = pl.pallas_call(
        functools.partial(_block_body, has_mix=mix is not None, has_proj=proj is not None),
        out_shape=out_shape,
        grid=(bsz, seq // ROW_TILE),
        in_specs=specs,
        out_specs=out_specs,
        scratch_shapes=[pltpu.VMEM((ROW_TILE, D_MODEL), _F32),
                        pltpu.VMEM((ROW_TILE, D_MODEL), _BF16),
                        pltpu.VMEM((ROW_TILE, D_MODEL), _F32)],
        compiler_params=_params(("arbitrary", "arbitrary")),
        name="block" + ("_mix" if mix is not None else "") + ("_proj" if proj is not None else ""),
    )(*args)
    return outs if proj is not None else (outs[0], None)


def _memkv_body(mem_ref, g_ref, w_ref, o_ref):
    mn = _rms(mem_ref[0], g_ref[...]).astype(_BF16)
    o_ref[0] = _dot(mn, w_ref[...])


def _memkv(mem, g, w):
    bsz = mem.shape[0]
    return pl.pallas_call(
        _memkv_body,
        out_shape=jax.ShapeDtypeStruct((bsz, N_MEM, 2 * MEM_W), _F32),
        grid=(bsz,),
        in_specs=[
            pl.BlockSpec((1, N_MEM, D_MODEL), lambda b: (b, 0, 0)),
            pl.BlockSpec((1, D_MODEL), lambda b: (0, 0)),
            pl.BlockSpec((D_MODEL, 2 * MEM_W), lambda b: (0, 0)),
        ],
        out_specs=pl.BlockSpec((1, N_MEM, 2 * MEM_W), lambda b: (b, 0, 0)),
        compiler_params=_params(("arbitrary",)),
        name="memkv",
    )(mem, g, w)


def _mem_block_diag(kvm):
    bsz = kvm.shape[0]
    k = kvm[:, :, :MEM_W].reshape(bsz, N_MEM, MEM_HEADS, HEAD_DIM)
    v = kvm[:, :, MEM_W:].reshape(bsz, N_MEM, MEM_HEADS, HEAD_DIM)
    eye = jnp.eye(MEM_HEADS, dtype=_F32)
    kbd = jnp.einsum("bmhd,hg->bhdgm", k * (1.0 / math.sqrt(HEAD_DIM)), eye)
    kbd = kbd.reshape(bsz, MEM_W, MEM_HEADS * N_MEM)
    vbd = jnp.einsum("bmhd,hg->bgmhd", v, eye).reshape(bsz, MEM_HEADS * N_MEM, MEM_W)
    return kbd.astype(_BF16), vbd.astype(_BF16)


def _rglru_body(xrec_ref, xgate_ref, cw_ref, cb_ref, wg_ref, gb_ref, lam_ref, o_ref,
                xbuf_ref, h_ref):
    s = pl.program_id(1)

    @pl.when(s == 0)
    def _():
        xbuf_ref[0:SUBLANES, :] = jnp.zeros((SUBLANES, REC_W), _F32)
        h_ref[...] = jnp.zeros_like(h_ref)

    xbuf_ref[SUBLANES:, :] = xrec_ref[0]
    xc = cb_ref[...] + cw_ref[CONV_W - 1:CONV_W, :] * xbuf_ref[SUBLANES:, :]
    for k in range(CONV_W - 1):
        off = SUBLANES - (CONV_W - 1) + k
        xc = xc + cw_ref[k:k + 1, :] * xbuf_ref[off:off + SEQ_TILE, :]
    xbuf_ref[0:SUBLANES, :] = xbuf_ref[SEQ_TILE:SEQ_TILE + SUBLANES, :]

    gates = _dot(xc.astype(_BF16), wg_ref[...]) + gb_ref[...]
    r = jax.nn.sigmoid(gates[:, :REC_W])
    i = jax.nn.sigmoid(gates[:, REC_W:])
    log_a = (-LRU_C) * r * _softplus(-lam_ref[...])
    a = jnp.exp(log_a)
    b = jnp.sqrt(-jnp.tanh(log_a) * (1.0 + a * a)) * (i * xc)

    n_groups = SEQ_TILE // SUBLANES
    a3 = a.reshape(n_groups, SUBLANES, REC_W)
    b3 = b.reshape(n_groups, SUBLANES, REC_W)
    sub = lax.broadcasted_iota(jnp.int32, (n_groups, SUBLANES, REC_W), 1)
    d = 1
    while d < SUBLANES:
        keep = sub >= d
        a_prev = jnp.where(keep, pltpu.roll(a3, d, 1), 1.0)
        b_prev = jnp.where(keep, pltpu.roll(b3, d, 1), 0.0)
        b3 = a3 * b_prev + b3
        a3 = a3 * a_prev
        d *= 2
    h = h_ref[...]
    outs = []
    for g in range(n_groups):
        hg = a3[g] * h + b3[g]
        outs.append(hg)
        h = hg[SUBLANES - 1:SUBLANES, :]
    h_ref[...] = h
    h_all = jnp.concatenate(outs, axis=0)
    o_ref[0] = (jax.nn.gelu(xgate_ref[0], approximate=True) * h_all).astype(o_ref.dtype)


def _rglru(proj, conv_w, conv_b, wgate, gate_b, lam):
    bsz, seq, _ = proj.shape
    return pl.pallas_call(
        _rglru_body,
        out_shape=jax.ShapeDtypeStruct((bsz, seq, REC_W), _BF16),
        grid=(bsz, seq // SEQ_TILE),
        in_specs=[
            pl.BlockSpec((1, SEQ_TILE, REC_W), lambda b, s: (b, s, 0)),
            pl.BlockSpec((1, SEQ_TILE, REC_W), lambda b, s: (b, s, 1)),
            _resident((CONV_W, REC_W)),
            _resident((1, REC_W)),
            _resident((REC_W, 2 * REC_W)),
            _resident((1, 2 * REC_W)),
            _resident((1, REC_W)),
        ],
        out_specs=pl.BlockSpec((1, SEQ_TILE, REC_W), lambda b, s: (b, s, 0)),
        scratch_shapes=[pltpu.VMEM((SEQ_TILE + SUBLANES, REC_W), _F32),
                        pltpu.VMEM((1, REC_W), _F32)],
        compiler_params=_params(("arbitrary", "arbitrary")),
        name="rglru",
    )(proj, proj, conv_w, conv_b, wgate, gate_b, lam)


def _gate_block_diag(gate_w):
    n_blocks = gate_w.shape[1]
    eye = jnp.eye(n_blocks, dtype=gate_w.dtype)
    bd = jnp.einsum("gncd,nm->gncmd", gate_w, eye).reshape(2, REC_W, REC_W)
    return jnp.concatenate([bd[0], bd[1]], axis=1).astype(_BF16)


def _sb_body(q_ref, k_ref, v_ref, o_ref, acc_ref, rest_ref):
    qb = pl.program_id(2)
    both = 2 * ATT_TILE
    lane = lax.broadcasted_iota(jnp.int32, (ATT_TILE, HEAD_PAIR), 1)
    row = lax.broadcasted_iota(jnp.int32, (ATT_TILE, ATT_TILE), 0)
    col = lax.broadcasted_iota(jnp.int32, (ATT_TILE, ATT_TILE), 1)
    suffix_ones = (row >= col).astype(_BF16)
    row2 = lax.broadcasted_iota(jnp.int32, (both, ATT_TILE), 0)
    col2 = lax.broadcasted_iota(jnp.int32, (both, ATT_TILE), 1)
    causal = col2 < jnp.where(row2 >= ATT_TILE, row2 - ATT_TILE, row2)
    scale = jnp.asarray(1.0 / math.sqrt(HEAD_DIM), _BF16)

    q_heads = []
    for u in range(ATT_SUB):
        q = q_ref[0, u * ATT_TILE:(u + 1) * ATT_TILE, :] * scale
        zero = jnp.zeros_like(q)
        q_heads.append(jnp.concatenate([jnp.where(lane < HEAD_DIM, q, zero),
                                        jnp.where(lane < HEAD_DIM, zero, q)], axis=0))

    def sweep(i, diagonal):
        starts, scores, softs = [], [], []
        for u in range(ATT_SUB):
            j = ATT_SUB * qb + u - i
            start = pl.multiple_of(jnp.maximum(j, 0) * ATT_TILE, ATT_TILE)
            kb = k_ref[0, pl.ds(start, ATT_TILE), :]
            z = lax.dot_general(q_heads[u], kb, (((1,), (1,)), ((), ())), preferred_element_type=_F32)
            if diagonal:
                z = jnp.where(causal, z, MASKED_SCORE)
            soft = jnp.maximum(z, 0.0) + jnp.log(1.0 + jnp.exp2(jnp.abs(z) * (-LOG2_E)))
            starts.append((j, start))
            scores.append(z)
            softs.append(soft.astype(_BF16))
        csum_all = _dot(jnp.concatenate(softs, axis=0), suffix_ones)
        least = None
        for u in range(ATT_SUB):
            j, start = starts[u]
            csum = csum_all[u * both:(u + 1) * both, :]
            total = jnp.broadcast_to(csum[:, 0:1], (both, HEAD_PAIR))
            vb = v_ref[0, pl.ds(start, ATT_TILE), :]
            if diagonal:
                w = jnp.exp(scores[u] - csum)
                acc_ref[u] = _dot(w.astype(_BF16), vb)
                rest = total
            else:
                rest = jnp.where(j >= 0, rest_ref[u], -MASKED_SCORE)
                w = jnp.exp(scores[u] - csum - jnp.concatenate([rest] * (ATT_TILE // HEAD_PAIR), axis=1))
                acc_ref[u] += _dot(w.astype(_BF16), vb)
                rest = rest + total
            rest_ref[u] = rest
            least = rest if least is None else jnp.minimum(least, rest)
        return jnp.min(least) < -F32_EXP_ZERO_BELOW

    more = sweep(0, True)
    last = ATT_SUB * qb + ATT_SUB - 1

    def cond(carry):
        i, more = carry
        return jnp.logical_and(i <= last, more)

    def body(carry):
        i, _ = carry
        return i + 1, sweep(i, False)

    lax.while_loop(cond, body, (jnp.int32(1), more))
    for u in range(ATT_SUB):
        out = jnp.where(lane < HEAD_DIM, acc_ref[u, 0:ATT_TILE, :], acc_ref[u, ATT_TILE:both, :])
        o_ref[0, u * ATT_TILE:(u + 1) * ATT_TILE, :] = out.astype(o_ref.dtype)


def _sb_attention(proj, kv):
    bsz, seq, _ = proj.shape
    n_pairs = SB_W // HEAD_PAIR
    return pl.pallas_call(
        _sb_body,
        out_shape=jax.ShapeDtypeStruct((bsz, seq, SB_W), _BF16),
        grid=(bsz, n_pairs, seq // ATT_QBLOCK),
        in_specs=[
            pl.BlockSpec((1, ATT_QBLOCK, HEAD_PAIR), lambda b, p, i: (b, i, p)),
            pl.BlockSpec((1, seq, HEAD_PAIR), lambda b, p, i: (b, 0, p)),
            pl.BlockSpec((1, seq, HEAD_PAIR), lambda b, p, i: (b, 0, n_pairs + p)),
        ],
        out_specs=pl.BlockSpec((1, ATT_QBLOCK, HEAD_PAIR), lambda b, p, i: (b, i, p)),
        scratch_shapes=[pltpu.VMEM((ATT_SUB, 2 * ATT_TILE, HEAD_PAIR), _F32),
                        pltpu.VMEM((ATT_SUB, 2 * ATT_TILE, HEAD_PAIR), _F32)],
        compiler_params=_params(("arbitrary", "arbitrary", "arbitrary")),
        name="sb_attention",
    )(proj, kv, kv)


def kernel(x, mem, ffn_w_gate, ffn_w_up, ffn_w_down, norm_g, mem_norm_g, w_mem_kv, w_mix_out,
           a_w_in, a_conv_w, a_conv_b, a_gate_w, a_gate_b, a_lambda, b_w_in, kv_norm_g,
           w_kv_shared):
    row1 = lambda v: v.reshape(1, -1).astype(_F32)
    bf = lambda w: w.astype(_BF16)

    def ffn_w(layer, half):
        return (bf(ffn_w_gate[layer, half]), bf(ffn_w_up[layer, half]), bf(ffn_w_down[layer, half]))

    def mem_kv(layer):
        return _mem_block_diag(_memkv(mem, row1(mem_norm_g[layer]), bf(w_mem_kv[layer])))

    x, proj = _block(x, ffn_w(0, 0), row1(norm_g[0, 0]), row1(norm_g[0, 1]),
                     proj=(row1(norm_g[0, 2]), bf(a_w_in[0]), _F32))
    y_main = _rglru(proj, a_conv_w[0], row1(a_conv_b[0]), _gate_block_diag(a_gate_w[0]),
                    row1(a_gate_b[0]), row1(a_lambda[0]))
    kbd, vbd = mem_kv(0)
    x, kv = _block(x, ffn_w(0, 1), row1(norm_g[0, 4]), row1(norm_g[0, 5]),
                   mix=(y_main, proj, (2 * REC_W) // MEM_W, kbd, vbd, bf(w_mix_out[0]), row1(norm_g[0, 3])),
                   proj=(row1(kv_norm_g), bf(w_kv_shared), _BF16))

    x, proj = _block(x, ffn_w(1, 0), row1(norm_g[1, 0]), row1(norm_g[1, 1]),
                     proj=(row1(norm_g[1, 2]), bf(b_w_in[0]), _BF16))
    y_main = _sb_attention(proj, kv)
    kbd, vbd = mem_kv(1)
    x, _ = _block(x, ffn_w(1, 1), row1(norm_g[1, 4]), row1(norm_g[1, 5]),
                  mix=(y_main, proj, SB_W // MEM_W, kbd, vbd, bf(w_mix_out[1]), row1(norm_g[1, 3])))
    return x
```

```python
import functools
import math

import jax
import jax.numpy as jnp
from jax import lax
from jax.experimental import pallas as pl
from jax.experimental.pallas import tpu as pltpu

D_MODEL = 1024
HEAD_DIM = 64
MEM_HEADS = 4
MEM_W = MEM_HEADS * HEAD_DIM
REC_W = D_MODEL - MEM_W
SB_W = D_MODEL - MEM_W
N_MEM = 256
CONV_W = 4
LRU_C = 8.0
D_FF = 2816
EPS = 1e-6

SUBLANES = 8
MXU_DIM = 256
VMEM_LIMIT_BYTES = 56 * 1024 * 1024

ROW_TILE = 512
FF_CHUNK = MXU_DIM
N_FF_CHUNKS = D_FF // FF_CHUNK
PREP_AFTER_CHUNK = (1, 5)
PREP_ROWS = ROW_TILE // len(PREP_AFTER_CHUNK)
SEQ_TILE = 512
ATT_TILE = 256
ATT_QBLOCK = 1024
ATT_SUB = ATT_QBLOCK // ATT_TILE
HEAD_PAIR = 2 * HEAD_DIM
MASKED_SCORE = -1e30
LOG2_E = math.log2(math.e)

F32_EXP_ZERO_BELOW = -110.0

_F32 = jnp.float32
_BF16 = jnp.bfloat16


def _params(semantics):
    return pltpu.CompilerParams(dimension_semantics=semantics, vmem_limit_bytes=VMEM_LIMIT_BYTES)


def _resident(shape):
    zeros = (0,) * len(shape)
    return pl.BlockSpec(shape, lambda *_: zeros, pipeline_mode=pl.Buffered(1))


def _rms(x, g):
    ms = jnp.mean(x * x, axis=-1, keepdims=True)
    return x * lax.rsqrt(ms + EPS) * g


def _softplus(z):
    return jnp.maximum(z, 0.0) + jnp.log(1.0 + jnp.exp(-jnp.abs(z)))


def _dot(a, b):
    return jnp.dot(a, b, preferred_element_type=_F32)


def _mem_attention(qm, kbd, vbd):
    scores = _dot(qm, kbd)
    probs = []
    for h in range(MEM_HEADS):
        sh = scores[:, h * N_MEM:(h + 1) * N_MEM]
        e = jnp.exp(sh - jnp.max(sh, axis=-1, keepdims=True))
        probs.append(e / jnp.sum(e, axis=-1, keepdims=True))
    p = jnp.concatenate(probs, axis=-1).astype(_BF16)
    return _dot(p, vbd)


def _block_body(*refs, has_mix, has_proj):
    refs = list(refs)
    x_ref = refs.pop(0)
    if has_mix:
        ymain_ref, qmem_ref, kbd_ref, vbd_ref, wout_ref, gmix_ref = refs[:6]
        del refs[:6]
    gin_ref, gout_ref, wg_ref, wu_ref, wd_ref = refs[:5]
    del refs[:5]
    if has_proj:
        gp_ref, wp_ref = refs[:2]
        del refs[:2]
    o_ref = refs.pop(0)
    if has_proj:
        p_ref = refs.pop(0)
    xres_refs, xn_refs, acc_ref = refs[0:2], refs[2:4], refs[4]

    t = pl.program_id(0)

    @pl.when(t == 0)
    def _():
        xres_refs[1][...] = jnp.zeros_like(xres_refs[1])
        xn_refs[1][...] = jnp.zeros_like(xn_refs[1])

    def prepare(nxt, rows):
        x = x_ref[0, rows, :]
        if has_mix:
            ymem = _mem_attention(qmem_ref[0, rows, :].astype(_BF16), kbd_ref[0], vbd_ref[0]).astype(_BF16)
            mix = _dot(ymain_ref[0, rows, :], wout_ref[0:SB_W, :]) + _dot(ymem, wout_ref[SB_W:, :])
            x = x + _rms(mix, gmix_ref[...])
        xres_refs[nxt][rows, :] = x
        xn_refs[nxt][rows, :] = _rms(x, gin_ref[...]).astype(_BF16)

    def step(cur, nxt):
        xn_ref = xn_refs[cur]
        for c in range(N_FF_CHUNKS):
            cols = slice(c * FF_CHUNK, (c + 1) * FF_CHUNK)
            gate = _dot(xn_ref[...], wg_ref[:, cols])
            up = _dot(xn_ref[...], wu_ref[:, cols])
            act = (gate * jax.nn.sigmoid(gate) * up).astype(_BF16)
            down = _dot(act, wd_ref[cols, :])
            if c == 0:
                acc_ref[...] = down
            else:
                acc_ref[...] += down
            if c in PREP_AFTER_CHUNK:
                slab = PREP_AFTER_CHUNK.index(c)
                prepare(nxt, slice(slab * PREP_ROWS, (slab + 1) * PREP_ROWS))
        xo = xres_refs[cur][...] + 0.5 * _rms(acc_ref[...], gout_ref[...])
        o_ref[0] = xo
        if has_proj:
            p_ref[0] = _dot(_rms(xo, gp_ref[...]).astype(_BF16), wp_ref[...]).astype(p_ref.dtype)

    parity = lax.rem(t, 2)
    pl.when(parity == 0)(functools.partial(step, 1, 0))
    pl.when(parity == 1)(functools.partial(step, 0, 1))


def _block(x, ffn_w, ffn_at, g_in, g_out, mix=None, proj=None):
    bsz, seq, _ = x.shape
    per_seq = seq // ROW_TILE
    n_tiles = bsz * per_seq

    def fed(t):
        t = jnp.minimum(t, n_tiles - 1)
        return t // per_seq, t % per_seq

    def done(t):
        t = jnp.maximum(t - 1, 0)
        return t // per_seq, t % per_seq

    def row_spec(width, tile_of, col=0):
        return pl.BlockSpec((1, ROW_TILE, width), lambda t: (*tile_of(t), col))

    per_batch = lambda a: pl.BlockSpec((1,) + a.shape[1:], lambda t: (fed(t)[0], 0, 0))
    gain = _resident((1, D_MODEL))

    args, specs = [x], [row_spec(D_MODEL, fed)]
    if mix is not None:
        y_main, q_src, q_block, kbd, vbd, w_out, g_mix = mix
        args += [y_main, q_src, kbd, vbd, w_out, g_mix]
        specs += [row_spec(SB_W, fed), row_spec(MEM_W, fed, q_block), per_batch(kbd), per_batch(vbd),
                  _resident(w_out.shape), gain]
    args += [g_in, g_out, *ffn_w]
    specs += [gain, gain] + [
        pl.BlockSpec((None, None) + w.shape[2:], lambda t: (*ffn_at, 0, 0), pipeline_mode=pl.Buffered(1))
        for w in ffn_w]
    out_shape = [jax.ShapeDtypeStruct((bsz, seq, D_MODEL), _F32)]
    out_specs = [row_spec(D_MODEL, done)]
    if proj is not None:
        g_proj, w_proj, p_dtype = proj
        args += [g_proj, w_proj]
        specs += [gain, _resident(w_proj.shape)]
        out_shape.append(jax.ShapeDtypeStruct((bsz, seq, w_proj.shape[1]), p_dtype))
        out_specs.append(row_spec(w_proj.shape[1], done))

    outs = pl.pallas_call(
        functools.partial(_block_body, has_mix=mix is not None, has_proj=proj is not None),
        out_shape=out_shape,
        grid=(n_tiles + 1,),
        in_specs=specs,
        out_specs=out_specs,
        scratch_shapes=[pltpu.VMEM((ROW_TILE, D_MODEL), _F32),
                        pltpu.VMEM((ROW_TILE, D_MODEL), _F32),
                        pltpu.VMEM((ROW_TILE, D_MODEL), _BF16),
                        pltpu.VMEM((ROW_TILE, D_MODEL), _BF16),
                        pltpu.VMEM((ROW_TILE, D_MODEL), _F32)],
        compiler_params=_params(("arbitrary",)),
        name="block" + ("_mix" if mix is not None else "") + ("_proj" if proj is not None else ""),
    )(*args)
    return outs if proj is not None else (outs[0], None)


def _memkv_body(mem_ref, g_ref, w_ref, o_ref):
    mn = _rms(mem_ref[0], g_ref[...]).astype(_BF16)
    o_ref[0] = _dot(mn, w_ref[...])


def _memkv(mem, g, w):
    bsz = mem.shape[0]
    return pl.pallas_call(
        _memkv_body,
        out_shape=jax.ShapeDtypeStruct((bsz, N_MEM, 2 * MEM_W), _F32),
        grid=(bsz,),
        in_specs=[
            pl.BlockSpec((1, N_MEM, D_MODEL), lambda b: (b, 0, 0)),
            pl.BlockSpec((1, D_MODEL), lambda b: (0, 0)),
            pl.BlockSpec((D_MODEL, 2 * MEM_W), lambda b: (0, 0)),
        ],
        out_specs=pl.BlockSpec((1, N_MEM, 2 * MEM_W), lambda b: (b, 0, 0)),
        compiler_params=_params(("arbitrary",)),
        name="memkv",
    )(mem, g, w)


def _mem_block_diag(kvm):
    bsz = kvm.shape[0]
    k = kvm[:, :, :MEM_W].reshape(bsz, N_MEM, MEM_HEADS, HEAD_DIM)
    v = kvm[:, :, MEM_W:].reshape(bsz, N_MEM, MEM_HEADS, HEAD_DIM)
    eye = jnp.eye(MEM_HEADS, dtype=_F32)
    kbd = jnp.einsum("bmhd,hg->bhdgm", k * (1.0 / math.sqrt(HEAD_DIM)), eye)
    kbd = kbd.reshape(bsz, MEM_W, MEM_HEADS * N_MEM)
    vbd = jnp.einsum("bmhd,hg->bgmhd", v, eye).reshape(bsz, MEM_HEADS * N_MEM, MEM_W)
    return kbd.astype(_BF16), vbd.astype(_BF16)


def _rglru_body(xrec_ref, xgate_ref, cw_ref, cb_ref, wg_ref, gb_ref, lam_ref, o_ref,
                xbuf_ref, h_ref):
    s = pl.program_id(1)

    @pl.when(s == 0)
    def _():
        xbuf_ref[0:SUBLANES, :] = jnp.zeros((SUBLANES, REC_W), _F32)
        h_ref[...] = jnp.zeros_like(h_ref)

    xbuf_ref[SUBLANES:, :] = xrec_ref[0]
    xc = cb_ref[...] + cw_ref[CONV_W - 1:CONV_W, :] * xbuf_ref[SUBLANES:, :]
    for k in range(CONV_W - 1):
        off = SUBLANES - (CONV_W - 1) + k
        xc = xc + cw_ref[k:k + 1, :] * xbuf_ref[off:off + SEQ_TILE, :]
    xbuf_ref[0:SUBLANES, :] = xbuf_ref[SEQ_TILE:SEQ_TILE + SUBLANES, :]

    gates = _dot(xc.astype(_BF16), wg_ref[...]) + gb_ref[...]
    r = jax.nn.sigmoid(gates[:, :REC_W])
    i = jax.nn.sigmoid(gates[:, REC_W:])
    log_a = (-LRU_C) * r * _softplus(-lam_ref[...])
    a = jnp.exp(log_a)
    var = -jnp.tanh(log_a) * (1.0 + a * a)
    b = jnp.where(var > 0.0, var * lax.rsqrt(var), 0.0) * (i * xc)

    n_groups = SEQ_TILE // SUBLANES
    a3 = a.reshape(n_groups, SUBLANES, REC_W)
    b3 = b.reshape(n_groups, SUBLANES, REC_W)
    sub = lax.broadcasted_iota(jnp.int32, (n_groups, SUBLANES, REC_W), 1)
    d = 1
    while d < SUBLANES:
        keep = sub >= d
        a_prev = jnp.where(keep, pltpu.roll(a3, d, 1), 1.0)
        b_prev = jnp.where(keep, pltpu.roll(b3, d, 1), 0.0)
        b3 = a3 * b_prev + b3
        a3 = a3 * a_prev
        d *= 2
    h = h_ref[...]
    outs = []
    for g in range(n_groups):
        hg = a3[g] * h + b3[g]
        outs.append(hg)
        h = hg[SUBLANES - 1:SUBLANES, :]
    h_ref[...] = h
    h_all = jnp.concatenate(outs, axis=0)
    o_ref[0] = (jax.nn.gelu(xgate_ref[0], approximate=True) * h_all).astype(o_ref.dtype)


def _rglru(proj, conv_w, conv_b, wgate, gate_b, lam):
    bsz, seq, _ = proj.shape
    return pl.pallas_call(
        _rglru_body,
        out_shape=jax.ShapeDtypeStruct((bsz, seq, REC_W), _BF16),
        grid=(bsz, seq // SEQ_TILE),
        in_specs=[
            pl.BlockSpec((1, SEQ_TILE, REC_W), lambda b, s: (b, s, 0)),
            pl.BlockSpec((1, SEQ_TILE, REC_W), lambda b, s: (b, s, 1)),
            _resident((CONV_W, REC_W)),
            _resident((1, REC_W)),
            _resident((REC_W, 2 * REC_W)),
            _resident((1, 2 * REC_W)),
            _resident((1, REC_W)),
        ],
        out_specs=pl.BlockSpec((1, SEQ_TILE, REC_W), lambda b, s: (b, s, 0)),
        scratch_shapes=[pltpu.VMEM((SEQ_TILE + SUBLANES, REC_W), _F32),
                        pltpu.VMEM((1, REC_W), _F32)],
        compiler_params=_params(("arbitrary", "arbitrary")),
        name="rglru",
    )(proj, proj, conv_w, conv_b, wgate, gate_b, lam)


def _gate_block_diag(gate_w):
    n_blocks = gate_w.shape[1]
    eye = jnp.eye(n_blocks, dtype=gate_w.dtype)
    bd = jnp.einsum("gncd,nm->gncmd", gate_w, eye).reshape(2, REC_W, REC_W)
    return jnp.concatenate([bd[0], bd[1]], axis=1).astype(_BF16)


def _sb_body(q_ref, k_ref, v_ref, o_ref, acc_ref, rest_ref):
    qb = pl.program_id(2)
    both = 2 * ATT_TILE
    lane = lax.broadcasted_iota(jnp.int32, (ATT_TILE, HEAD_PAIR), 1)
    row = lax.broadcasted_iota(jnp.int32, (ATT_TILE, ATT_TILE), 0)
    col = lax.broadcasted_iota(jnp.int32, (ATT_TILE, ATT_TILE), 1)
    suffix_ones = (row >= col).astype(_BF16)
    row2 = lax.broadcasted_iota(jnp.int32, (both, ATT_TILE), 0)
    col2 = lax.broadcasted_iota(jnp.int32, (both, ATT_TILE), 1)
    causal = col2 < jnp.where(row2 >= ATT_TILE, row2 - ATT_TILE, row2)
    scale = jnp.asarray(1.0 / math.sqrt(HEAD_DIM), _BF16)

    q_heads = []
    for u in range(ATT_SUB):
        q = q_ref[0, u * ATT_TILE:(u + 1) * ATT_TILE, :] * scale
        zero = jnp.zeros_like(q)
        q_heads.append(jnp.concatenate([jnp.where(lane < HEAD_DIM, q, zero),
                                        jnp.where(lane < HEAD_DIM, zero, q)], axis=0))

    def sweep(i, diagonal):
        starts, scores, softs = [], [], []
        for u in range(ATT_SUB):
            j = ATT_SUB * qb + u - i
            start = pl.multiple_of(jnp.maximum(j, 0) * ATT_TILE, ATT_TILE)
            kb = k_ref[0, pl.ds(start, ATT_TILE), :]
            z = lax.dot_general(q_heads[u], kb, (((1,), (1,)), ((), ())), preferred_element_type=_F32)
            if diagonal:
                z = jnp.where(causal, z, MASKED_SCORE)
            soft = jnp.maximum(z, 0.0) + jnp.log(1.0 + jnp.exp2(jnp.abs(z) * (-LOG2_E)))
            starts.append((j, start))
            scores.append(z)
            softs.append(soft.astype(_BF16))
        csum_all = _dot(jnp.concatenate(softs, axis=0), suffix_ones)
        least = None
        for u in range(ATT_SUB):
            j, start = starts[u]
            csum = csum_all[u * both:(u + 1) * both, :]
            total = jnp.broadcast_to(csum[:, 0:1], (both, HEAD_PAIR))
            vb = v_ref[0, pl.ds(start, ATT_TILE), :]
            if diagonal:
                w = jnp.exp(scores[u] - csum)
                acc_ref[u] = _dot(w.astype(_BF16), vb)
                rest = total
            else:
                rest = jnp.where(j >= 0, rest_ref[u], -MASKED_SCORE)
                w = jnp.exp(scores[u] - csum - jnp.concatenate([rest] * (ATT_TILE // HEAD_PAIR), axis=1))
                acc_ref[u] += _dot(w.astype(_BF16), vb)
                rest = rest + total
            rest_ref[u] = rest
            least = rest if least is None else jnp.minimum(least, rest)
        return jnp.min(least) < -F32_EXP_ZERO_BELOW

    more = sweep(0, True)
    last = ATT_SUB * qb + ATT_SUB - 1

    def cond(carry):
        i, more = carry
        return jnp.logical_and(i <= last, more)

    def body(carry):
        i, _ = carry
        return i + 1, sweep(i, False)

    lax.while_loop(cond, body, (jnp.int32(1), more))
    for u in range(ATT_SUB):
        out = jnp.where(lane < HEAD_DIM, acc_ref[u, 0:ATT_TILE, :], acc_ref[u, ATT_TILE:both, :])
        o_ref[0, u * ATT_TILE:(u + 1) * ATT_TILE, :] = out.astype(o_ref.dtype)


def _sb_attention(proj, kv):
    bsz, seq, _ = proj.shape
    n_pairs = SB_W // HEAD_PAIR
    return pl.pallas_call(
        _sb_body,
        out_shape=jax.ShapeDtypeStruct((bsz, seq, SB_W), _BF16),
        grid=(bsz, n_pairs, seq // ATT_QBLOCK),
        in_specs=[
            pl.BlockSpec((1, ATT_QBLOCK, HEAD_PAIR), lambda b, p, i: (b, i, p)),
            pl.BlockSpec((1, seq, HEAD_PAIR), lambda b, p, i: (b, 0, p)),
            pl.BlockSpec((1, seq, HEAD_PAIR), lambda b, p, i: (b, 0, n_pairs + p)),
        ],
        out_specs=pl.BlockSpec((1, ATT_QBLOCK, HEAD_PAIR), lambda b, p, i: (b, i, p)),
        scratch_shapes=[pltpu.VMEM((ATT_SUB, 2 * ATT_TILE, HEAD_PAIR), _F32),
                        pltpu.VMEM((ATT_SUB, 2 * ATT_TILE, HEAD_PAIR), _F32)],
        compiler_params=_params(("arbitrary", "arbitrary", "arbitrary")),
        name="sb_attention",
    )(proj, kv, kv)


def kernel(x, mem, ffn_w_gate, ffn_w_up, ffn_w_down, norm_g, mem_norm_g, w_mem_kv, w_mix_out,
           a_w_in, a_conv_w, a_conv_b, a_gate_w, a_gate_b, a_lambda, b_w_in, kv_norm_g,
           w_kv_shared):
    row1 = lambda v: v.reshape(1, -1).astype(_F32)
    bf = lambda w: w.astype(_BF16)

    ffn_w = (bf(ffn_w_gate), bf(ffn_w_up), bf(ffn_w_down))

    def mem_kv(layer):
        return _mem_block_diag(_memkv(mem, row1(mem_norm_g[layer]), bf(w_mem_kv[layer])))

    x, proj = _block(x, ffn_w, (0, 0), row1(norm_g[0, 0]), row1(norm_g[0, 1]),
                     proj=(row1(norm_g[0, 2]), bf(a_w_in[0]), _F32))
    y_main = _rglru(proj, a_conv_w[0], row1(a_conv_b[0]), _gate_block_diag(a_gate_w[0]),
                    row1(a_gate_b[0]), row1(a_lambda[0]))
    kbd, vbd = mem_kv(0)
    x, kv = _block(x, ffn_w, (0, 1), row1(norm_g[0, 4]), row1(norm_g[0, 5]),
                   mix=(y_main, proj, (2 * REC_W) // MEM_W, kbd, vbd, bf(w_mix_out[0]), row1(norm_g[0, 3])),
                   proj=(row1(kv_norm_g), bf(w_kv_shared), _BF16))

    x, proj = _block(x, ffn_w, (1, 0), row1(norm_g[1, 0]), row1(norm_g[1, 1]),
                     proj=(row1(norm_g[1, 2]), bf(b_w_in[0]), _BF16))
    y_main = _sb_attention(proj, kv)
    kbd, vbd = mem_kv(1)
    x, _ = _block(x, ffn_w, (1, 1), row1(norm_g[1, 4]), row1(norm_g[1, 5]),
                  mix=(y_main, proj, SB_W // MEM_W, kbd, vbd, bf(w_mix_out[1]), row1(norm_g[1, 3])))
    return x
```

```python
import functools
import math

import jax
import jax.numpy as jnp
from jax import lax
from jax.experimental import pallas as pl
from jax.experimental.pallas import tpu as pltpu

D_MODEL = 1024
HEAD_DIM = 64
MEM_HEADS = 4
MEM_W = MEM_HEADS * HEAD_DIM
REC_W = D_MODEL - MEM_W
SB_W = D_MODEL - MEM_W
N_MEM = 256
CONV_W = 4
LRU_C = 8.0
D_FF = 2816
EPS = 1e-6

SUBLANES = 8
MXU_DIM = 256
VMEM_LIMIT_BYTES = 60 * 1024 * 1024

ROW_TILE = 512
FF_CHUNK = MXU_DIM
N_FF_CHUNKS = D_FF // FF_CHUNK
PREP_AFTER_CHUNK = (1, 5)
PREP_ROWS = ROW_TILE // len(PREP_AFTER_CHUNK)
ATT_TILE = 256
ATT_QBLOCK = 1024
ATT_SUB = ATT_QBLOCK // ATT_TILE
HEAD_PAIR = 2 * HEAD_DIM
MASKED_SCORE = -1e30
LOG2_E = math.log2(math.e)

F32_EXP_ZERO_BELOW = -110.0

_F32 = jnp.float32
_BF16 = jnp.bfloat16


def _params(semantics):
    return pltpu.CompilerParams(dimension_semantics=semantics, vmem_limit_bytes=VMEM_LIMIT_BYTES)


def _resident(shape):
    zeros = (0,) * len(shape)
    return pl.BlockSpec(shape, lambda *_: zeros, pipeline_mode=pl.Buffered(1))


def _rms(x, g):
    ms = jnp.mean(x * x, axis=-1, keepdims=True)
    return x * lax.rsqrt(ms + EPS) * g


def _softplus(z):
    return jnp.maximum(z, 0.0) + jnp.log(1.0 + jnp.exp(-jnp.abs(z)))


def _dot(a, b):
    return jnp.dot(a, b, preferred_element_type=_F32)


def _mem_attention(qm, kbd, vbd):
    scores = _dot(qm, kbd)
    probs = []
    for h in range(MEM_HEADS):
        sh = scores[:, h * N_MEM:(h + 1) * N_MEM]
        e = jnp.exp(sh - jnp.max(sh, axis=-1, keepdims=True))
        probs.append(e / jnp.sum(e, axis=-1, keepdims=True))
    p = jnp.concatenate(probs, axis=-1).astype(_BF16)
    return _dot(p, vbd)


def _block_body(*refs, has_mix, has_rec, has_proj, n_tiles, per_seq):
    refs = list(refs)
    x_ref = refs.pop(0)
    if has_rec:
        xrec_ref, xgate_ref = refs[:2]
        rec_refs = refs[2:7]
        del refs[:7]
    elif has_mix:
        ymain_ref = refs.pop(0)
    if has_mix:
        qmem_ref, kbd_ref, vbd_ref, wout_ref, gmix_ref = refs[:5]
        del refs[:5]
    gin_ref, gout_ref, wg_ref, wu_ref, wd_ref = refs[:5]
    del refs[:5]
    if has_proj:
        gp_ref, wp_ref = refs[:2]
        del refs[:2]
    o_ref = refs.pop(0)
    if has_proj:
        p_ref = refs.pop(0)
    if has_mix:
        xres_refs, xn_refs, acc_ref = refs[0:2], refs[2:4], refs[4]
        del refs[:5]
    else:
        xres_refs, xn_refs, acc_ref = refs[0:1], refs[1:2], refs[2]
        del refs[:3]
    if has_rec:
        xbuf_ref, h_ref = refs

    t = pl.program_id(0)

    def prepare(slot, rows, tile):
        x = x_ref[0, rows, :]
        if has_mix:
            if has_rec:
                first = (tile % per_seq == 0) if rows.start == 0 else False
                ymain = _rglru_rows(xrec_ref[0, rows, :], xgate_ref[0, rows, :], first, *rec_refs,
                                    xbuf_ref, h_ref).astype(_BF16)
            else:
                ymain = ymain_ref[0, rows, :]
            ymem = _mem_attention(qmem_ref[0, rows, :].astype(_BF16), kbd_ref[0], vbd_ref[0]).astype(_BF16)
            mix = _dot(ymain, wout_ref[0:SB_W, :]) + _dot(ymem, wout_ref[SB_W:, :])
            x = x + _rms(mix, gmix_ref[...])
        xres_refs[slot][rows, :] = x
        xn_refs[slot][rows, :] = _rms(x, gin_ref[...]).astype(_BF16)

    def ffn(slot, between=None):
        xn_ref = xn_refs[slot]
        for c in range(N_FF_CHUNKS):
            cols = slice(c * FF_CHUNK, (c + 1) * FF_CHUNK)
            gate = _dot(xn_ref[...], wg_ref[:, cols])
            up = _dot(xn_ref[...], wu_ref[:, cols])
            act = (gate * jax.nn.sigmoid(gate) * up).astype(_BF16)
            down = _dot(act, wd_ref[cols, :])
            if c == 0:
                acc_ref[...] = down
            else:
                acc_ref[...] += down
            if between is not None and c in PREP_AFTER_CHUNK:
                between(PREP_AFTER_CHUNK.index(c))

    def finish(slot):
        xo = xres_refs[slot][...] + 0.5 * _rms(acc_ref[...], gout_ref[...])
        o_ref[0] = xo
        if has_proj:
            p_ref[0] = _dot(_rms(xo, gp_ref[...]).astype(_BF16), wp_ref[...]).astype(p_ref.dtype)

    slab_rows = lambda slab: slice(slab * PREP_ROWS, (slab + 1) * PREP_ROWS)

    if not has_mix:
        prepare(0, slice(0, ROW_TILE), t)
        ffn(0)
        finish(0)
        return

    def first_step():
        if has_rec:
            xbuf_ref[...] = jnp.zeros_like(xbuf_ref)
            h_ref[...] = jnp.zeros_like(h_ref)
        for slab in range(len(PREP_AFTER_CHUNK)):
            prepare(0, slab_rows(slab), t)

    def middle_step(cur, nxt):
        ffn(cur, between=lambda slab: prepare(nxt, slab_rows(slab), t))
        finish(cur)

    def last_step():
        ffn(1)
        finish(1)

    odd = lax.rem(t, 2) == 1
    pl.when(t == 0)(first_step)
    pl.when(odd)(functools.partial(middle_step, 0, 1))
    pl.when((t > 0) & (t < n_tiles) & jnp.logical_not(odd))(functools.partial(middle_step, 1, 0))
    pl.when(t == n_tiles)(last_step)


def _block(x, ffn_w, ffn_at, g_in, g_out, mix=None, proj=None):
    bsz, seq, _ = x.shape
    per_seq = seq // ROW_TILE
    n_tiles = bsz * per_seq
    has_mix = mix is not None
    has_rec = has_mix and isinstance(mix[0], tuple)
    assert n_tiles % 2 == 0

    if has_mix:
        fed = lambda t: jnp.minimum(t, n_tiles - 1)
        done = lambda t: jnp.maximum(t - 1, 0)
        n_steps = n_tiles + 1
    else:
        fed = done = lambda t: t
        n_steps = n_tiles

    def row_spec(width, tile_of, col=0):
        return pl.BlockSpec((1, ROW_TILE, width),
                            lambda t: (tile_of(t) // per_seq, tile_of(t) % per_seq, col))

    per_batch = lambda a: pl.BlockSpec((1,) + a.shape[1:], lambda t: (fed(t) // per_seq, 0, 0))
    gain = _resident((1, D_MODEL))

    args, specs = [x], [row_spec(D_MODEL, fed)]
    scratch = []
    if has_mix:
        y_src, q_src, q_block, kbd, vbd, w_out, g_mix = mix
        if has_rec:
            rec_in, *rec_params = y_src
            args += [rec_in, rec_in, *rec_params]
            specs += [row_spec(REC_W, fed, 0), row_spec(REC_W, fed, 1)] + [_resident(p.shape) for p in rec_params]
            scratch = [pltpu.VMEM((PREP_ROWS + SUBLANES, REC_W), _F32), pltpu.VMEM((1, REC_W), _F32)]
        else:
            args += [y_src]
            specs += [row_spec(SB_W, fed)]
        args += [q_src, kbd, vbd, w_out, g_mix]
        specs += [row_spec(MEM_W, fed, q_block), per_batch(kbd), per_batch(vbd), _resident(w_out.shape), gain]
    args += [g_in, g_out, *ffn_w]
    specs += [gain, gain] + [
        pl.BlockSpec((None, None) + w.shape[2:], lambda t: (*ffn_at, 0, 0), pipeline_mode=pl.Buffered(1))
        for w in ffn_w]
    out_shape = [jax.ShapeDtypeStruct((bsz, seq, D_MODEL), _F32)]
    out_specs = [row_spec(D_MODEL, done)]
    if proj is not None:
        g_proj, w_proj, p_dtype = proj
        args += [g_proj, w_proj]
        specs += [gain, _resident(w_proj.shape)]
        out_shape.append(jax.ShapeDtypeStruct((bsz, seq, w_proj.shape[1]), p_dtype))
        out_specs.append(row_spec(w_proj.shape[1], done))

    slots = 2 if has_mix else 1
    outs = pl.pallas_call(
        functools.partial(_block_body, has_mix=has_mix, has_rec=has_rec, has_proj=proj is not None,
                          n_tiles=n_tiles, per_seq=per_seq),
        out_shape=out_shape,
        grid=(n_steps,),
        in_specs=specs,
        out_specs=out_specs,
        scratch_shapes=([pltpu.VMEM((ROW_TILE, D_MODEL), _F32)] * slots
                        + [pltpu.VMEM((ROW_TILE, D_MODEL), _BF16)] * slots
                        + [pltpu.VMEM((ROW_TILE, D_MODEL), _F32)] + scratch),
        compiler_params=_params(("arbitrary",)),
        name="block" + ("_rec" if has_rec else "") + ("_mix" if has_mix else "") + ("_proj" if proj is not None else ""),
    )(*args)
    return outs if proj is not None else (outs[0], None)


def _memkv_body(mem_ref, g_ref, w_ref, o_ref):
    mn = _rms(mem_ref[0], g_ref[...]).astype(_BF16)
    o_ref[0] = _dot(mn, w_ref[...])


def _memkv(mem, g, w):
    bsz = mem.shape[0]
    return pl.pallas_call(
        _memkv_body,
        out_shape=jax.ShapeDtypeStruct((bsz, N_MEM, 2 * MEM_W), _F32),
        grid=(bsz,),
        in_specs=[
            pl.BlockSpec((1, N_MEM, D_MODEL), lambda b: (b, 0, 0)),
            pl.BlockSpec((1, D_MODEL), lambda b: (0, 0)),
            pl.BlockSpec((D_MODEL, 2 * MEM_W), lambda b: (0, 0)),
        ],
        out_specs=pl.BlockSpec((1, N_MEM, 2 * MEM_W), lambda b: (b, 0, 0)),
        compiler_params=_params(("arbitrary",)),
        name="memkv",
    )(mem, g, w)


def _mem_block_diag(kvm):
    bsz = kvm.shape[0]
    k = kvm[:, :, :MEM_W].reshape(bsz, N_MEM, MEM_HEADS, HEAD_DIM)
    v = kvm[:, :, MEM_W:].reshape(bsz, N_MEM, MEM_HEADS, HEAD_DIM)
    eye = jnp.eye(MEM_HEADS, dtype=_F32)
    kbd = jnp.einsum("bmhd,hg->bhdgm", k * (1.0 / math.sqrt(HEAD_DIM)), eye)
    kbd = kbd.reshape(bsz, MEM_W, MEM_HEADS * N_MEM)
    vbd = jnp.einsum("bmhd,hg->bgmhd", v, eye).reshape(bsz, MEM_HEADS * N_MEM, MEM_W)
    return kbd.astype(_BF16), vbd.astype(_BF16)


def _rglru_rows(xr, xg, first, cw_ref, cb_ref, wg_ref, gb_ref, lam_ref, xbuf_ref, h_ref):
    rows = xr.shape[0]
    tail = xbuf_ref[rows:rows + SUBLANES, :]
    xbuf_ref[0:SUBLANES, :] = jnp.where(first, 0.0, tail)
    xbuf_ref[SUBLANES:, :] = xr
    xc = cb_ref[...] + cw_ref[CONV_W - 1:CONV_W, :] * xr
    for k in range(CONV_W - 1):
        off = SUBLANES - (CONV_W - 1) + k
        xc = xc + cw_ref[k:k + 1, :] * xbuf_ref[off:off + rows, :]

    gates = _dot(xc.astype(_BF16), wg_ref[...]) + gb_ref[...]
    r = jax.nn.sigmoid(gates[:, :REC_W])
    i = jax.nn.sigmoid(gates[:, REC_W:])
    log_a = (-LRU_C) * r * _softplus(-lam_ref[...])
    a = jnp.exp(log_a)
    var = -jnp.tanh(log_a) * (1.0 + a * a)
    b = jnp.where(var > 0.0, var * lax.rsqrt(var), 0.0) * (i * xc)

    n_groups = rows // SUBLANES
    a3 = a.reshape(n_groups, SUBLANES, REC_W)
    b3 = b.reshape(n_groups, SUBLANES, REC_W)
    sub = lax.broadcasted_iota(jnp.int32, (n_groups, SUBLANES, REC_W), 1)
    d = 1
    while d < SUBLANES:
        keep = sub >= d
        a_prev = jnp.where(keep, pltpu.roll(a3, d, 1), 1.0)
        b_prev = jnp.where(keep, pltpu.roll(b3, d, 1), 0.0)
        b3 = a3 * b_prev + b3
        a3 = a3 * a_prev
        d *= 2
    h = jnp.where(first, 0.0, h_ref[...])
    outs = []
    for g in range(n_groups):
        hg = a3[g] * h + b3[g]
        outs.append(hg)
        h = hg[SUBLANES - 1:SUBLANES, :]
    h_ref[...] = h
    h_all = jnp.concatenate(outs, axis=0)
    return jax.nn.gelu(xg, approximate=True) * h_all


def _gate_block_diag(gate_w):
    n_blocks = gate_w.shape[1]
    eye = jnp.eye(n_blocks, dtype=gate_w.dtype)
    bd = jnp.einsum("gncd,nm->gncmd", gate_w, eye).reshape(2, REC_W, REC_W)
    return jnp.concatenate([bd[0], bd[1]], axis=1).astype(_BF16)


def _sb_body(q_ref, k_ref, v_ref, o_ref, acc_ref, rest_ref):
    qb = pl.program_id(2)
    both = 2 * ATT_TILE
    lane = lax.broadcasted_iota(jnp.int32, (ATT_TILE, HEAD_PAIR), 1)
    row = lax.broadcasted_iota(jnp.int32, (ATT_TILE, ATT_TILE), 0)
    col = lax.broadcasted_iota(jnp.int32, (ATT_TILE, ATT_TILE), 1)
    suffix_ones = (row >= col).astype(_BF16)
    row2 = lax.broadcasted_iota(jnp.int32, (both, ATT_TILE), 0)
    col2 = lax.broadcasted_iota(jnp.int32, (both, ATT_TILE), 1)
    causal = col2 < jnp.where(row2 >= ATT_TILE, row2 - ATT_TILE, row2)
    scale = jnp.asarray(1.0 / math.sqrt(HEAD_DIM), _BF16)

    q_heads = []
    for u in range(ATT_SUB):
        q = q_ref[0, u * ATT_TILE:(u + 1) * ATT_TILE, :] * scale
        zero = jnp.zeros_like(q)
        q_heads.append(jnp.concatenate([jnp.where(lane < HEAD_DIM, q, zero),
                                        jnp.where(lane < HEAD_DIM, zero, q)], axis=0))

    def sweep(i, diagonal):
        starts, scores, softs = [], [], []
        for u in range(ATT_SUB):
            j = ATT_SUB * qb + u - i
            start = pl.multiple_of(jnp.maximum(j, 0) * ATT_TILE, ATT_TILE)
            kb = k_ref[0, pl.ds(start, ATT_TILE), :]
            z = lax.dot_general(q_heads[u], kb, (((1,), (1,)), ((), ())), preferred_element_type=_F32)
            if diagonal:
                z = jnp.where(causal, z, MASKED_SCORE)
            soft = jnp.maximum(z, 0.0) + jnp.log(1.0 + jnp.exp2(jnp.abs(z) * (-LOG2_E)))
            starts.append((j, start))
            scores.append(z)
            softs.append(soft.astype(_BF16))
        csum_all = _dot(jnp.concatenate(softs, axis=0), suffix_ones)
        least = None
        for u in range(ATT_SUB):
            j, start = starts[u]
            csum = csum_all[u * both:(u + 1) * both, :]
            total = jnp.broadcast_to(csum[:, 0:1], (both, HEAD_PAIR))
            vb = v_ref[0, pl.ds(start, ATT_TILE), :]
            if diagonal:
                w = jnp.exp(scores[u] - csum)
                acc_ref[u] = _dot(w.astype(_BF16), vb)
                rest = total
            else:
                rest = jnp.where(j >= 0, rest_ref[u], -MASKED_SCORE)
                w = jnp.exp(scores[u] - csum - jnp.concatenate([rest] * (ATT_TILE // HEAD_PAIR), axis=1))
                acc_ref[u] += _dot(w.astype(_BF16), vb)
                rest = rest + total
            rest_ref[u] = rest
            least = rest if least is None else jnp.minimum(least, rest)
        return jnp.min(least) < -F32_EXP_ZERO_BELOW

    more = sweep(0, True)
    last = ATT_SUB * qb + ATT_SUB - 1

    def cond(carry):
        i, more = carry
        return jnp.logical_and(i <= last, more)

    def body(carry):
        i, _ = carry
        return i + 1, sweep(i, False)

    lax.while_loop(cond, body, (jnp.int32(1), more))
    for u in range(ATT_SUB):
        out = jnp.where(lane < HEAD_DIM, acc_ref[u, 0:ATT_TILE, :], acc_ref[u, ATT_TILE:both, :])
        o_ref[0, u * ATT_TILE:(u + 1) * ATT_TILE, :] = out.astype(o_ref.dtype)


def _sb_attention(proj, kv):
    bsz, seq, _ = proj.shape
    n_pairs = SB_W // HEAD_PAIR
    return pl.pallas_call(
        _sb_body,
        out_shape=jax.ShapeDtypeStruct((bsz, seq, SB_W), _BF16),
        grid=(bsz, n_pairs, seq // ATT_QBLOCK),
        in_specs=[
            pl.BlockSpec((1, ATT_QBLOCK, HEAD_PAIR), lambda b, p, i: (b, i, p)),
            pl.BlockSpec((1, seq, HEAD_PAIR), lambda b, p, i: (b, 0, p)),
            pl.BlockSpec((1, seq, HEAD_PAIR), lambda b, p, i: (b, 0, n_pairs + p)),
        ],
        out_specs=pl.BlockSpec((1, ATT_QBLOCK, HEAD_PAIR), lambda b, p, i: (b, i, p)),
        scratch_shapes=[pltpu.VMEM((ATT_SUB, 2 * ATT_TILE, HEAD_PAIR), _F32),
                        pltpu.VMEM((ATT_SUB, 2 * ATT_TILE, HEAD_PAIR), _F32)],
        compiler_params=_params(("arbitrary", "arbitrary", "arbitrary")),
        name="sb_attention",
    )(proj, kv, kv)


def kernel(x, mem, ffn_w_gate, ffn_w_up, ffn_w_down, norm_g, mem_norm_g, w_mem_kv, w_mix_out,
           a_w_in, a_conv_w, a_conv_b, a_gate_w, a_gate_b, a_lambda, b_w_in, kv_norm_g,
           w_kv_shared):
    row1 = lambda v: v.reshape(1, -1).astype(_F32)
    bf = lambda w: w.astype(_BF16)

    ffn_w = (bf(ffn_w_gate), bf(ffn_w_up), bf(ffn_w_down))

    def mem_kv(layer):
        return _mem_block_diag(_memkv(mem, row1(mem_norm_g[layer]), bf(w_mem_kv[layer])))

    x, proj = _block(x, ffn_w, (0, 0), row1(norm_g[0, 0]), row1(norm_g[0, 1]),
                     proj=(row1(norm_g[0, 2]), bf(a_w_in[0]), _F32))
    rec = (proj, a_conv_w[0], row1(a_conv_b[0]), _gate_block_diag(a_gate_w[0]), row1(a_gate_b[0]),
           row1(a_lambda[0]))
    kbd, vbd = mem_kv(0)
    x, kv = _block(x, ffn_w, (0, 1), row1(norm_g[0, 4]), row1(norm_g[0, 5]),
                   mix=(rec, proj, (2 * REC_W) // MEM_W, kbd, vbd, bf(w_mix_out[0]), row1(norm_g[0, 3])),
                   proj=(row1(kv_norm_g), bf(w_kv_shared), _BF16))

    x, proj = _block(x, ffn_w, (1, 0), row1(norm_g[1, 0]), row1(norm_g[1, 1]),
                     proj=(row1(norm_g[1, 2]), bf(b_w_in[0]), _BF16))
    y_main = _sb_attention(proj, kv)
    kbd, vbd = mem_kv(1)
    x, _ = _block(x, ffn_w, (1, 1), row1(norm_g[1, 4]), row1(norm_g[1, 5]),
                  mix=(y_main, proj, SB_W // MEM_W, kbd, vbd, bf(w_mix_out[1]), row1(norm_g[1, 3])))
    return x
```

```python
import functools
import math

import jax
import jax.numpy as jnp
from jax import lax
from jax.experimental import pallas as pl
from jax.experimental.pallas import tpu as pltpu

D_MODEL = 1024
HEAD_DIM = 64
MEM_HEADS = 4
MEM_W = MEM_HEADS * HEAD_DIM
REC_W = D_MODEL - MEM_W
SB_W = D_MODEL - MEM_W
N_MEM = 256
CONV_W = 4
LRU_C = 8.0
D_FF = 2816
EPS = 1e-6

SUBLANES = 8
MXU_DIM = 256
VMEM_LIMIT_BYTES = 60 * 1024 * 1024

ROW_TILE = 512
FF_CHUNK = MXU_DIM
N_FF_CHUNKS = D_FF // FF_CHUNK
PREP_AFTER_CHUNK = (1, 5)
PREP_ROWS = ROW_TILE // len(PREP_AFTER_CHUNK)
ATT_TILE = 256
ATT_QBLOCK = 1024
ATT_SUB = ATT_QBLOCK // ATT_TILE
HEAD_PAIR = 2 * HEAD_DIM
MASKED_SCORE = -1e30
LOG2_E = math.log2(math.e)

F32_EXP_ZERO_BELOW = -110.0

_F32 = jnp.float32
_BF16 = jnp.bfloat16


def _params(semantics):
    return pltpu.CompilerParams(dimension_semantics=semantics, vmem_limit_bytes=VMEM_LIMIT_BYTES)


def _resident(shape):
    zeros = (0,) * len(shape)
    return pl.BlockSpec(shape, lambda *_: zeros, pipeline_mode=pl.Buffered(1))


def _rms(x, g):
    ms = jnp.mean(x * x, axis=-1, keepdims=True)
    return x * lax.rsqrt(ms + EPS) * g


def _softplus(z):
    return jnp.maximum(z, 0.0) + jnp.log(1.0 + jnp.exp(-jnp.abs(z)))


def _dot(a, b):
    return jnp.dot(a, b, preferred_element_type=_F32)


def _mem_attention(qm, kbd, vbd):
    scores = _dot(qm, kbd)
    probs = []
    for h in range(MEM_HEADS):
        sh = scores[:, h * N_MEM:(h + 1) * N_MEM]
        e = jnp.exp(sh - jnp.max(sh, axis=-1, keepdims=True))
        probs.append(e / jnp.sum(e, axis=-1, keepdims=True))
    p = jnp.concatenate(probs, axis=-1).astype(_BF16)
    return _dot(p, vbd)


def _block_body(*refs, has_mix, has_rec, has_proj, n_tiles, per_seq):
    refs = list(refs)
    x_ref = refs.pop(0)
    if has_rec:
        xrec_ref, xgate_ref = refs[:2]
        rec_refs = refs[2:7]
        del refs[:7]
    elif has_mix:
        ymain_ref = refs.pop(0)
    if has_mix:
        qmem_ref, kbd_ref, vbd_ref, wout_ref, gmix_ref = refs[:5]
        del refs[:5]
    gin_ref, gout_ref, wg_ref, wu_ref, wd_ref = refs[:5]
    del refs[:5]
    if has_proj:
        gp_ref, wp_ref = refs[:2]
        del refs[:2]
    o_ref = refs.pop(0)
    if has_proj:
        p_ref = refs.pop(0)
    if has_mix:
        xres_refs, xn_refs, acc_ref = refs[0:2], refs[2:4], refs[4]
        del refs[:5]
    else:
        xres_refs, xn_refs, acc_ref = refs[0:1], refs[1:2], refs[2]
        del refs[:3]
    if has_rec:
        xbuf_ref, h_ref = refs

    t = pl.program_id(0)

    def prepare(slot, rows, tile):
        x = x_ref[0, rows, :]
        if has_mix:
            if has_rec:
                first = (tile % per_seq == 0) if rows.start == 0 else False
                ymain = _rglru_rows(xrec_ref[0, rows, :], xgate_ref[0, rows, :], first, *rec_refs,
                                    xbuf_ref, h_ref).astype(_BF16)
            else:
                ymain = ymain_ref[0, rows, :]
            ymem = _mem_attention(qmem_ref[0, rows, :].astype(_BF16), kbd_ref[0], vbd_ref[0]).astype(_BF16)
            mix = _dot(ymain, wout_ref[0:SB_W, :]) + _dot(ymem, wout_ref[SB_W:, :])
            x = x + _rms(mix, gmix_ref[...])
        xres_refs[slot][rows, :] = x
        xn_refs[slot][rows, :] = _rms(x, gin_ref[...]).astype(_BF16)

    def ffn(slot, between=None):
        xn_ref = xn_refs[slot]
        for c in range(N_FF_CHUNKS):
            cols = slice(c * FF_CHUNK, (c + 1) * FF_CHUNK)
            gate = _dot(xn_ref[...], wg_ref[:, cols])
            up = _dot(xn_ref[...], wu_ref[:, cols])
            act = (gate * jax.nn.sigmoid(gate) * up).astype(_BF16)
            down = _dot(act, wd_ref[cols, :])
            if c == 0:
                acc_ref[...] = down
            else:
                acc_ref[...] += down
            if between is not None and c in PREP_AFTER_CHUNK:
                between(PREP_AFTER_CHUNK.index(c))

    def finish(slot):
        xo = xres_refs[slot][...] + 0.5 * _rms(acc_ref[...], gout_ref[...])
        o_ref[0] = xo
        if has_proj:
            p_ref[0] = _dot(_rms(xo, gp_ref[...]).astype(_BF16), wp_ref[...]).astype(p_ref.dtype)

    slab_rows = lambda slab: slice(slab * PREP_ROWS, (slab + 1) * PREP_ROWS)

    if not has_mix:
        prepare(0, slice(0, ROW_TILE), t)
        ffn(0)
        finish(0)
        return

    def first_step():
        if has_rec:
            xbuf_ref[...] = jnp.zeros_like(xbuf_ref)
            h_ref[...] = jnp.zeros_like(h_ref)
        for slab in range(len(PREP_AFTER_CHUNK)):
            prepare(1, slab_rows(slab), t)

    def later_step(prepare_next):
        xn_refs[0][...] = xn_refs[1][...]
        xres_refs[0][...] = xres_refs[1][...]
        ffn(0, between=(lambda slab: prepare(1, slab_rows(slab), t)) if prepare_next else None)
        finish(0)

    pl.when(t == 0)(first_step)
    pl.when((t > 0) & (t < n_tiles))(functools.partial(later_step, True))
    pl.when(t == n_tiles)(functools.partial(later_step, False))


def _block(x, ffn_w, ffn_at, g_in, g_out, mix=None, proj=None):
    bsz, seq, _ = x.shape
    per_seq = seq // ROW_TILE
    n_tiles = bsz * per_seq
    has_mix = mix is not None
    has_rec = has_mix and isinstance(mix[0], tuple)

    if has_mix:
        fed = lambda t: jnp.minimum(t, n_tiles - 1)
        done = lambda t: jnp.maximum(t - 1, 0)
        n_steps = n_tiles + 1
    else:
        fed = done = lambda t: t
        n_steps = n_tiles

    def row_spec(width, tile_of, col=0):
        return pl.BlockSpec((1, ROW_TILE, width),
                            lambda t: (tile_of(t) // per_seq, tile_of(t) % per_seq, col))

    per_batch = lambda a: pl.BlockSpec((1,) + a.shape[1:], lambda t: (fed(t) // per_seq, 0, 0))
    gain = _resident((1, D_MODEL))

    args, specs = [x], [row_spec(D_MODEL, fed)]
    scratch = []
    if has_mix:
        y_src, q_src, q_block, kbd, vbd, w_out, g_mix = mix
        if has_rec:
            rec_in, *rec_params = y_src
            args += [rec_in, rec_in, *rec_params]
            specs += [row_spec(REC_W, fed, 0), row_spec(REC_W, fed, 1)] + [_resident(p.shape) for p in rec_params]
            scratch = [pltpu.VMEM((PREP_ROWS + SUBLANES, REC_W), _F32), pltpu.VMEM((1, REC_W), _F32)]
        else:
            args += [y_src]
            specs += [row_spec(SB_W, fed)]
        args += [q_src, kbd, vbd, w_out, g_mix]
        specs += [row_spec(MEM_W, fed, q_block), per_batch(kbd), per_batch(vbd), _resident(w_out.shape), gain]
    args += [g_in, g_out, *ffn_w]
    specs += [gain, gain] + [
        pl.BlockSpec((None, None) + w.shape[2:], lambda t: (*ffn_at, 0, 0), pipeline_mode=pl.Buffered(1))
        for w in ffn_w]
    out_shape = [jax.ShapeDtypeStruct((bsz, seq, D_MODEL), _F32)]
    out_specs = [row_spec(D_MODEL, done)]
    if proj is not None:
        g_proj, w_proj, p_dtype = proj
        args += [g_proj, w_proj]
        specs += [gain, _resident(w_proj.shape)]
        out_shape.append(jax.ShapeDtypeStruct((bsz, seq, w_proj.shape[1]), p_dtype))
        out_specs.append(row_spec(w_proj.shape[1], done))

    slots = 2 if has_mix else 1
    outs = pl.pallas_call(
        functools.partial(_block_body, has_mix=has_mix, has_rec=has_rec, has_proj=proj is not None,
                          n_tiles=n_tiles, per_seq=per_seq),
        out_shape=out_shape,
        grid=(n_steps,),
        in_specs=specs,
        out_specs=out_specs,
        scratch_shapes=([pltpu.VMEM((ROW_TILE, D_MODEL), _F32)] * slots
                        + [pltpu.VMEM((ROW_TILE, D_MODEL), _BF16)] * slots
                        + [pltpu.VMEM((ROW_TILE, D_MODEL), _F32)] + scratch),
        compiler_params=_params(("arbitrary",)),
        name="block" + ("_rec" if has_rec else "") + ("_mix" if has_mix else "") + ("_proj" if proj is not None else ""),
    )(*args)
    return outs if proj is not None else (outs[0], None)


def _memkv_body(mem_ref, g_ref, w_ref, o_ref):
    mn = _rms(mem_ref[0], g_ref[...]).astype(_BF16)
    o_ref[0] = _dot(mn, w_ref[...])


def _memkv(mem, g, w):
    bsz = mem.shape[0]
    return pl.pallas_call(
        _memkv_body,
        out_shape=jax.ShapeDtypeStruct((bsz, N_MEM, 2 * MEM_W), _F32),
        grid=(bsz,),
        in_specs=[
            pl.BlockSpec((1, N_MEM, D_MODEL), lambda b: (b, 0, 0)),
            pl.BlockSpec((1, D_MODEL), lambda b: (0, 0)),
            pl.BlockSpec((D_MODEL, 2 * MEM_W), lambda b: (0, 0)),
        ],
        out_specs=pl.BlockSpec((1, N_MEM, 2 * MEM_W), lambda b: (b, 0, 0)),
        compiler_params=_params(("arbitrary",)),
        name="memkv",
    )(mem, g, w)


def _mem_block_diag(kvm):
    bsz = kvm.shape[0]
    k = kvm[:, :, :MEM_W].reshape(bsz, N_MEM, MEM_HEADS, HEAD_DIM)
    v = kvm[:, :, MEM_W:].reshape(bsz, N_MEM, MEM_HEADS, HEAD_DIM)
    eye = jnp.eye(MEM_HEADS, dtype=_F32)
    kbd = jnp.einsum("bmhd,hg->bhdgm", k * (1.0 / math.sqrt(HEAD_DIM)), eye)
    kbd = kbd.reshape(bsz, MEM_W, MEM_HEADS * N_MEM)
    vbd = jnp.einsum("bmhd,hg->bgmhd", v, eye).reshape(bsz, MEM_HEADS * N_MEM, MEM_W)
    return kbd.astype(_BF16), vbd.astype(_BF16)


def _rglru_rows(xr, xg, first, cw_ref, cb_ref, wg_ref, gb_ref, lam_ref, xbuf_ref, h_ref):
    rows = xr.shape[0]
    tail = xbuf_ref[rows:rows + SUBLANES, :]
    xbuf_ref[0:SUBLANES, :] = jnp.where(first, 0.0, tail)
    xbuf_ref[SUBLANES:, :] = xr
    xc = cb_ref[...] + cw_ref[CONV_W - 1:CONV_W, :] * xr
    for k in range(CONV_W - 1):
        off = SUBLANES - (CONV_W - 1) + k
        xc = xc + cw_ref[k:k + 1, :] * xbuf_ref[off:off + rows, :]

    gates = _dot(xc.astype(_BF16), wg_ref[...]) + gb_ref[...]
    r = jax.nn.sigmoid(gates[:, :REC_W])
    i = jax.nn.sigmoid(gates[:, REC_W:])
    log_a = (-LRU_C) * r * _softplus(-lam_ref[...])
    a = jnp.exp(log_a)
    var = -jnp.tanh(log_a) * (1.0 + a * a)
    b = jnp.where(var > 0.0, var * lax.rsqrt(var), 0.0) * (i * xc)

    n_groups = rows // SUBLANES
    a3 = a.reshape(n_groups, SUBLANES, REC_W)
    b3 = b.reshape(n_groups, SUBLANES, REC_W)
    sub = lax.broadcasted_iota(jnp.int32, (n_groups, SUBLANES, REC_W), 1)
    d = 1
    while d < SUBLANES:
        keep = sub >= d
        a_prev = jnp.where(keep, pltpu.roll(a3, d, 1), 1.0)
        b_prev = jnp.where(keep, pltpu.roll(b3, d, 1), 0.0)
        b3 = a3 * b_prev + b3
        a3 = a3 * a_prev
        d *= 2
    h = jnp.where(first, 0.0, h_ref[...])
    outs = []
    for g in range(n_groups):
        hg = a3[g] * h + b3[g]
        outs.append(hg)
        h = hg[SUBLANES - 1:SUBLANES, :]
    h_ref[...] = h
    h_all = jnp.concatenate(outs, axis=0)
    return jax.nn.gelu(xg, approximate=True) * h_all


def _gate_block_diag(gate_w):
    n_blocks = gate_w.shape[1]
    eye = jnp.eye(n_blocks, dtype=gate_w.dtype)
    bd = jnp.einsum("gncd,nm->gncmd", gate_w, eye).reshape(2, REC_W, REC_W)
    return jnp.concatenate([bd[0], bd[1]], axis=1).astype(_BF16)


def _sb_body(q_ref, k_ref, v_ref, o_ref, acc_ref, rest_ref):
    qb = pl.program_id(2)
    both = 2 * ATT_TILE
    lane = lax.broadcasted_iota(jnp.int32, (ATT_TILE, HEAD_PAIR), 1)
    row = lax.broadcasted_iota(jnp.int32, (ATT_TILE, ATT_TILE), 0)
    col = lax.broadcasted_iota(jnp.int32, (ATT_TILE, ATT_TILE), 1)
    suffix_ones = (row >= col).astype(_BF16)
    row2 = lax.broadcasted_iota(jnp.int32, (both, ATT_TILE), 0)
    col2 = lax.broadcasted_iota(jnp.int32, (both, ATT_TILE), 1)
    causal = col2 < jnp.where(row2 >= ATT_TILE, row2 - ATT_TILE, row2)
    scale = jnp.asarray(1.0 / math.sqrt(HEAD_DIM), _BF16)

    q_heads = []
    for u in range(ATT_SUB):
        q = q_ref[0, u * ATT_TILE:(u + 1) * ATT_TILE, :] * scale
        zero = jnp.zeros_like(q)
        q_heads.append(jnp.concatenate([jnp.where(lane < HEAD_DIM, q, zero),
                                        jnp.where(lane < HEAD_DIM, zero, q)], axis=0))

    def sweep(i, diagonal):
        starts, scores, softs = [], [], []
        for u in range(ATT_SUB):
            j = ATT_SUB * qb + u - i
            start = pl.multiple_of(jnp.maximum(j, 0) * ATT_TILE, ATT_TILE)
            kb = k_ref[0, pl.ds(start, ATT_TILE), :]
            z = lax.dot_general(q_heads[u], kb, (((1,), (1,)), ((), ())), preferred_element_type=_F32)
            if diagonal:
                z = jnp.where(causal, z, MASKED_SCORE)
            soft = jnp.maximum(z, 0.0) + jnp.log(1.0 + jnp.exp2(jnp.abs(z) * (-LOG2_E)))
            starts.append((j, start))
            scores.append(z)
            softs.append(soft.astype(_BF16))
        csum_all = _dot(jnp.concatenate(softs, axis=0), suffix_ones)
        least = None
        for u in range(ATT_SUB):
            j, start = starts[u]
            csum = csum_all[u * both:(u + 1) * both, :]
            total = jnp.broadcast_to(csum[:, 0:1], (both, HEAD_PAIR))
            vb = v_ref[0, pl.ds(start, ATT_TILE), :]
            if diagonal:
                w = jnp.exp(scores[u] - csum)
                acc_ref[u] = _dot(w.astype(_BF16), vb)
                rest = total
            else:
                rest = jnp.where(j >= 0, rest_ref[u], -MASKED_SCORE)
                w = jnp.exp(scores[u] - csum - jnp.concatenate([rest] * (ATT_TILE // HEAD_PAIR), axis=1))
                acc_ref[u] += _dot(w.astype(_BF16), vb)
                rest = rest + total
            rest_ref[u] = rest
            least = rest if least is None else jnp.minimum(least, rest)
        return jnp.min(least) < -F32_EXP_ZERO_BELOW

    more = sweep(0, True)
    last = ATT_SUB * qb + ATT_SUB - 1

    def cond(carry):
        i, more = carry
        return jnp.logical_and(i <= last, more)

    def body(carry):
        i, _ = carry
        return i + 1, sweep(i, False)

    lax.while_loop(cond, body, (jnp.int32(1), more))
    for u in range(ATT_SUB):
        out = jnp.where(lane < HEAD_DIM, acc_ref[u, 0:ATT_TILE, :], acc_ref[u, ATT_TILE:both, :])
        o_ref[0, u * ATT_TILE:(u + 1) * ATT_TILE, :] = out.astype(o_ref.dtype)


def _sb_attention(proj, kv):
    bsz, seq, _ = proj.shape
    n_pairs = SB_W // HEAD_PAIR
    return pl.pallas_call(
        _sb_body,
        out_shape=jax.ShapeDtypeStruct((bsz, seq, SB_W), _BF16),
        grid=(bsz, n_pairs, seq // ATT_QBLOCK),
        in_specs=[
            pl.BlockSpec((1, ATT_QBLOCK, HEAD_PAIR), lambda b, p, i: (b, i, p)),
            pl.BlockSpec((1, seq, HEAD_PAIR), lambda b, p, i: (b, 0, p)),
            pl.BlockSpec((1, seq, HEAD_PAIR), lambda b, p, i: (b, 0, n_pairs + p)),
        ],
        out_specs=pl.BlockSpec((1, ATT_QBLOCK, HEAD_PAIR), lambda b, p, i: (b, i, p)),
        scratch_shapes=[pltpu.VMEM((ATT_SUB, 2 * ATT_TILE, HEAD_PAIR), _F32),
                        pltpu.VMEM((ATT_SUB, 2 * ATT_TILE, HEAD_PAIR), _F32)],
        compiler_params=_params(("arbitrary", "arbitrary", "arbitrary")),
        name="sb_attention",
    )(proj, kv, kv)


def kernel(x, mem, ffn_w_gate, ffn_w_up, ffn_w_down, norm_g, mem_norm_g, w_mem_kv, w_mix_out,
           a_w_in, a_conv_w, a_conv_b, a_gate_w, a_gate_b, a_lambda, b_w_in, kv_norm_g,
           w_kv_shared):
    row1 = lambda v: v.reshape(1, -1).astype(_F32)
    bf = lambda w: w.astype(_BF16)

    ffn_w = (bf(ffn_w_gate), bf(ffn_w_up), bf(ffn_w_down))

    def mem_kv(layer):
        return _mem_block_diag(_memkv(mem, row1(mem_norm_g[layer]), bf(w_mem_kv[layer])))

    x, proj = _block(x, ffn_w, (0, 0), row1(norm_g[0, 0]), row1(norm_g[0, 1]),
                     proj=(row1(norm_g[0, 2]), bf(a_w_in[0]), _F32))
    rec = (proj, a_conv_w[0], row1(a_conv_b[0]), _gate_block_diag(a_gate_w[0]), row1(a_gate_b[0]),
           row1(a_lambda[0]))
    kbd, vbd = mem_kv(0)
    x, kv = _block(x, ffn_w, (0, 1), row1(norm_g[0, 4]), row1(norm_g[0, 5]),
                   mix=(rec, proj, (2 * REC_W) // MEM_W, kbd, vbd, bf(w_mix_out[0]), row1(norm_g[0, 3])),
                   proj=(row1(kv_norm_g), bf(w_kv_shared), _BF16))

    x, proj = _block(x, ffn_w, (1, 0), row1(norm_g[1, 0]), row1(norm_g[1, 1]),
                     proj=(row1(norm_g[1, 2]), bf(b_w_in[0]), _BF16))
    y_main = _sb_attention(proj, kv)
    kbd, vbd = mem_kv(1)
    x, _ = _block(x, ffn_w, (1, 1), row1(norm_g[1, 4]), row1(norm_g[1, 5]),
                  mix=(y_main, proj, SB_W // MEM_W, kbd, vbd, bf(w_mix_out[1]), row1(norm_g[1, 3])))
    return x
```

```python
import functools
import math

import jax
import jax.numpy as jnp
from jax import lax
from jax.experimental import pallas as pl
from jax.experimental.pallas import tpu as pltpu

D_MODEL = 1024
HEAD_DIM = 64
MEM_HEADS = 4
MEM_W = MEM_HEADS * HEAD_DIM
REC_W = D_MODEL - MEM_W
SB_W = D_MODEL - MEM_W
N_MEM = 256
CONV_W = 4
LRU_C = 8.0
D_FF = 2816
EPS = 1e-6

SUBLANES = 8
MXU_DIM = 256
VMEM_LIMIT_BYTES = 60 * 1024 * 1024

ROW_TILE = 512
FF_CHUNK = MXU_DIM
N_FF_CHUNKS = D_FF // FF_CHUNK
PREP_AFTER_CHUNK = (1, 5)
PREP_ROWS = ROW_TILE // len(PREP_AFTER_CHUNK)
ATT_TILE = 256
ATT_QBLOCK = 2048
ATT_SUB = ATT_QBLOCK // ATT_TILE
HEAD_PAIR = 2 * HEAD_DIM
MASKED_SCORE = -1e30
LOG2_E = math.log2(math.e)

F32_EXP_ZERO_BELOW = -110.0

_F32 = jnp.float32
_BF16 = jnp.bfloat16


def _params(semantics):
    return pltpu.CompilerParams(dimension_semantics=semantics, vmem_limit_bytes=VMEM_LIMIT_BYTES)


def _resident(shape):
    zeros = (0,) * len(shape)
    return pl.BlockSpec(shape, lambda *_: zeros, pipeline_mode=pl.Buffered(1))


def _rms(x, g):
    ms = jnp.mean(x * x, axis=-1, keepdims=True)
    return x * lax.rsqrt(ms + EPS) * g


def _softplus(z):
    return jnp.maximum(z, 0.0) + jnp.log(1.0 + jnp.exp(-jnp.abs(z)))


def _dot(a, b):
    return jnp.dot(a, b, preferred_element_type=_F32)


def _mem_attention(qm, kbd, vbd):
    scores = _dot(qm, kbd)
    probs = []
    for h in range(MEM_HEADS):
        sh = scores[:, h * N_MEM:(h + 1) * N_MEM]
        e = jnp.exp(sh - jnp.max(sh, axis=-1, keepdims=True))
        probs.append(e / jnp.sum(e, axis=-1, keepdims=True))
    p = jnp.concatenate(probs, axis=-1).astype(_BF16)
    return _dot(p, vbd)


def _block_body(*refs, has_mix, has_rec, has_proj, n_tiles, per_seq):
    refs = list(refs)
    x_ref = refs.pop(0)
    if has_rec:
        xrec_ref, xgate_ref = refs[:2]
        rec_refs = refs[2:7]
        del refs[:7]
    elif has_mix:
        ymain_ref = refs.pop(0)
    if has_mix:
        qmem_ref, kbd_ref, vbd_ref, wout_ref, gmix_ref = refs[:5]
        del refs[:5]
    gin_ref, gout_ref, wg_ref, wu_ref, wd_ref = refs[:5]
    del refs[:5]
    if has_proj:
        gp_ref, wp_ref = refs[:2]
        del refs[:2]
    o_ref = refs.pop(0)
    if has_proj:
        p_ref = refs.pop(0)
    if has_mix:
        xres_refs, xn_refs, acc_ref = refs[0:2], refs[2:4], refs[4]
        del refs[:5]
    else:
        xres_refs, xn_refs, acc_ref = refs[0:1], refs[1:2], refs[2]
        del refs[:3]
    if has_rec:
        xbuf_ref, h_ref = refs

    t = pl.program_id(0)

    def prepare(slot, rows, tile):
        x = x_ref[0, rows, :]
        if has_mix:
            if has_rec:
                first = (tile % per_seq == 0) if rows.start == 0 else False
                ymain = _rglru_rows(xrec_ref[0, rows, :], xgate_ref[0, rows, :], first, *rec_refs,
                                    xbuf_ref, h_ref).astype(_BF16)
            else:
                ymain = ymain_ref[0, rows, :]
            ymem = _mem_attention(qmem_ref[0, rows, :].astype(_BF16), kbd_ref[0], vbd_ref[0]).astype(_BF16)
            mix = _dot(ymain, wout_ref[0:SB_W, :]) + _dot(ymem, wout_ref[SB_W:, :])
            x = x + _rms(mix, gmix_ref[...])
        xres_refs[slot][rows, :] = x
        xn_refs[slot][rows, :] = _rms(x, gin_ref[...]).astype(_BF16)

    def ffn(slot, between=None):
        xn_ref = xn_refs[slot]
        for c in range(N_FF_CHUNKS):
            cols = slice(c * FF_CHUNK, (c + 1) * FF_CHUNK)
            gate = _dot(xn_ref[...], wg_ref[:, cols])
            up = _dot(xn_ref[...], wu_ref[:, cols])
            act = (gate * jax.nn.sigmoid(gate) * up).astype(_BF16)
            down = _dot(act, wd_ref[cols, :])
            if c == 0:
                acc_ref[...] = down
            else:
                acc_ref[...] += down
            if between is not None and c in PREP_AFTER_CHUNK:
                between(PREP_AFTER_CHUNK.index(c))

    def finish(slot):
        xo = xres_refs[slot][...] + 0.5 * _rms(acc_ref[...], gout_ref[...])
        o_ref[0] = xo
        if has_proj:
            p_ref[0] = _dot(_rms(xo, gp_ref[...]).astype(_BF16), wp_ref[...]).astype(p_ref.dtype)

    slab_rows = lambda slab: slice(slab * PREP_ROWS, (slab + 1) * PREP_ROWS)

    if not has_mix:
        prepare(0, slice(0, ROW_TILE), t)
        ffn(0)
        finish(0)
        return

    def first_step():
        if has_rec:
            xbuf_ref[...] = jnp.zeros_like(xbuf_ref)
            h_ref[...] = jnp.zeros_like(h_ref)
        for slab in range(len(PREP_AFTER_CHUNK)):
            prepare(1, slab_rows(slab), t)

    def later_step(prepare_next):
        xn_refs[0][...] = xn_refs[1][...]
        xres_refs[0][...] = xres_refs[1][...]
        ffn(0, between=(lambda slab: prepare(1, slab_rows(slab), t)) if prepare_next else None)
        finish(0)

    pl.when(t == 0)(first_step)
    pl.when((t > 0) & (t < n_tiles))(functools.partial(later_step, True))
    pl.when(t == n_tiles)(functools.partial(later_step, False))


def _block(x, ffn_w, ffn_at, g_in, g_out, mix=None, proj=None):
    bsz, seq, _ = x.shape
    per_seq = seq // ROW_TILE
    n_tiles = bsz * per_seq
    has_mix = mix is not None
    has_rec = has_mix and isinstance(mix[0], tuple)

    if has_mix:
        fed = lambda t: jnp.minimum(t, n_tiles - 1)
        done = lambda t: jnp.maximum(t - 1, 0)
        n_steps = n_tiles + 1
    else:
        fed = done = lambda t: t
        n_steps = n_tiles

    def row_spec(width, tile_of, col=0):
        return pl.BlockSpec((1, ROW_TILE, width),
                            lambda t: (tile_of(t) // per_seq, tile_of(t) % per_seq, col))

    per_batch = lambda a: pl.BlockSpec((1,) + a.shape[1:], lambda t: (fed(t) // per_seq, 0, 0))
    gain = _resident((1, D_MODEL))

    args, specs = [x], [row_spec(D_MODEL, fed)]
    scratch = []
    if has_mix:
        y_src, q_src, q_block, kbd, vbd, w_out, g_mix = mix
        if has_rec:
            rec_in, *rec_params = y_src
            args += [rec_in, rec_in, *rec_params]
            specs += [row_spec(REC_W, fed, 0), row_spec(REC_W, fed, 1)] + [_resident(p.shape) for p in rec_params]
            scratch = [pltpu.VMEM((PREP_ROWS + SUBLANES, REC_W), _F32), pltpu.VMEM((1, REC_W), _F32)]
        else:
            args += [y_src]
            specs += [row_spec(SB_W, fed)]
        args += [q_src, kbd, vbd, w_out, g_mix]
        specs += [row_spec(MEM_W, fed, q_block), per_batch(kbd), per_batch(vbd), _resident(w_out.shape), gain]
    args += [g_in, g_out, *ffn_w]
    specs += [gain, gain] + [
        pl.BlockSpec((None, None) + w.shape[2:], lambda t: (*ffn_at, 0, 0), pipeline_mode=pl.Buffered(1))
        for w in ffn_w]
    out_shape = [jax.ShapeDtypeStruct((bsz, seq, D_MODEL), _F32)]
    out_specs = [row_spec(D_MODEL, done)]
    if proj is not None:
        g_proj, w_proj, p_dtype = proj
        args += [g_proj, w_proj]
        specs += [gain, _resident(w_proj.shape)]
        out_shape.append(jax.ShapeDtypeStruct((bsz, seq, w_proj.shape[1]), p_dtype))
        out_specs.append(row_spec(w_proj.shape[1], done))

    slots = 2 if has_mix else 1
    outs = pl.pallas_call(
        functools.partial(_block_body, has_mix=has_mix, has_rec=has_rec, has_proj=proj is not None,
                          n_tiles=n_tiles, per_seq=per_seq),
        out_shape=out_shape,
        grid=(n_steps,),
        in_specs=specs,
        out_specs=out_specs,
        scratch_shapes=([pltpu.VMEM((ROW_TILE, D_MODEL), _F32)] * slots
                        + [pltpu.VMEM((ROW_TILE, D_MODEL), _BF16)] * slots
                        + [pltpu.VMEM((ROW_TILE, D_MODEL), _F32)] + scratch),
        compiler_params=_params(("arbitrary",)),
        name="block" + ("_rec" if has_rec else "") + ("_mix" if has_mix else "") + ("_proj" if proj is not None else ""),
    )(*args)
    return outs if proj is not None else (outs[0], None)


def _memkv_body(mem_ref, g_ref, w_ref, o_ref):
    mn = _rms(mem_ref[0], g_ref[...]).astype(_BF16)
    o_ref[0] = _dot(mn, w_ref[...])


def _memkv(mem, g, w):
    bsz = mem.shape[0]
    return pl.pallas_call(
        _memkv_body,
        out_shape=jax.ShapeDtypeStruct((bsz, N_MEM, 2 * MEM_W), _F32),
        grid=(bsz,),
        in_specs=[
            pl.BlockSpec((1, N_MEM, D_MODEL), lambda b: (b, 0, 0)),
            pl.BlockSpec((1, D_MODEL), lambda b: (0, 0)),
            pl.BlockSpec((D_MODEL, 2 * MEM_W), lambda b: (0, 0)),
        ],
        out_specs=pl.BlockSpec((1, N_MEM, 2 * MEM_W), lambda b: (b, 0, 0)),
        compiler_params=_params(("arbitrary",)),
        name="memkv",
    )(mem, g, w)


def _mem_block_diag(kvm):
    bsz = kvm.shape[0]
    k = kvm[:, :, :MEM_W].reshape(bsz, N_MEM, MEM_HEADS, HEAD_DIM)
    v = kvm[:, :, MEM_W:].reshape(bsz, N_MEM, MEM_HEADS, HEAD_DIM)
    eye = jnp.eye(MEM_HEADS, dtype=_F32)
    kbd = jnp.einsum("bmhd,hg->bhdgm", k * (1.0 / math.sqrt(HEAD_DIM)), eye)
    kbd = kbd.reshape(bsz, MEM_W, MEM_HEADS * N_MEM)
    vbd = jnp.einsum("bmhd,hg->bgmhd", v, eye).reshape(bsz, MEM_HEADS * N_MEM, MEM_W)
    return kbd.astype(_BF16), vbd.astype(_BF16)


def _rglru_rows(xr, xg, first, cw_ref, cb_ref, wg_ref, gb_ref, lam_ref, xbuf_ref, h_ref):
    rows = xr.shape[0]
    tail = xbuf_ref[rows:rows + SUBLANES, :]
    xbuf_ref[0:SUBLANES, :] = jnp.where(first, 0.0, tail)
    xbuf_ref[SUBLANES:, :] = xr
    xc = cb_ref[...] + cw_ref[CONV_W - 1:CONV_W, :] * xr
    for k in range(CONV_W - 1):
        off = SUBLANES - (CONV_W - 1) + k
        xc = xc + cw_ref[k:k + 1, :] * xbuf_ref[off:off + rows, :]

    gates = _dot(xc.astype(_BF16), wg_ref[...]) + gb_ref[...]
    r = jax.nn.sigmoid(gates[:, :REC_W])
    i = jax.nn.sigmoid(gates[:, REC_W:])
    log_a = (-LRU_C) * r * _softplus(-lam_ref[...])
    a = jnp.exp(log_a)
    var = -jnp.tanh(log_a) * (1.0 + a * a)
    b = jnp.where(var > 0.0, var * lax.rsqrt(var), 0.0) * (i * xc)

    n_groups = rows // SUBLANES
    a3 = a.reshape(n_groups, SUBLANES, REC_W)
    b3 = b.reshape(n_groups, SUBLANES, REC_W)
    sub = lax.broadcasted_iota(jnp.int32, (n_groups, SUBLANES, REC_W), 1)
    d = 1
    while d < SUBLANES:
        keep = sub >= d
        a_prev = jnp.where(keep, pltpu.roll(a3, d, 1), 1.0)
        b_prev = jnp.where(keep, pltpu.roll(b3, d, 1), 0.0)
        b3 = a3 * b_prev + b3
        a3 = a3 * a_prev
        d *= 2
    h = jnp.where(first, 0.0, h_ref[...])
    outs = []
    for g in range(n_groups):
        hg = a3[g] * h + b3[g]
        outs.append(hg)
        h = hg[SUBLANES - 1:SUBLANES, :]
    h_ref[...] = h
    h_all = jnp.concatenate(outs, axis=0)
    return jax.nn.gelu(xg, approximate=True) * h_all


def _gate_block_diag(gate_w):
    n_blocks = gate_w.shape[1]
    eye = jnp.eye(n_blocks, dtype=gate_w.dtype)
    bd = jnp.einsum("gncd,nm->gncmd", gate_w, eye).reshape(2, REC_W, REC_W)
    return jnp.concatenate([bd[0], bd[1]], axis=1).astype(_BF16)


def _sb_body(q_ref, k_ref, v_ref, o_ref, acc_ref, rest_ref):
    qb = pl.program_id(2)
    both = 2 * ATT_TILE
    lane = lax.broadcasted_iota(jnp.int32, (ATT_TILE, HEAD_PAIR), 1)
    row = lax.broadcasted_iota(jnp.int32, (ATT_TILE, ATT_TILE), 0)
    col = lax.broadcasted_iota(jnp.int32, (ATT_TILE, ATT_TILE), 1)
    suffix_ones = (row >= col).astype(_BF16)
    row2 = lax.broadcasted_iota(jnp.int32, (both, ATT_TILE), 0)
    col2 = lax.broadcasted_iota(jnp.int32, (both, ATT_TILE), 1)
    causal = col2 < jnp.where(row2 >= ATT_TILE, row2 - ATT_TILE, row2)
    scale = jnp.asarray(1.0 / math.sqrt(HEAD_DIM), _BF16)

    q_heads = []
    for u in range(ATT_SUB):
        q = q_ref[0, u * ATT_TILE:(u + 1) * ATT_TILE, :] * scale
        zero = jnp.zeros_like(q)
        q_heads.append(jnp.concatenate([jnp.where(lane < HEAD_DIM, q, zero),
                                        jnp.where(lane < HEAD_DIM, zero, q)], axis=0))

    def sweep(i, diagonal):
        starts, scores, softs = [], [], []
        for u in range(ATT_SUB):
            j = ATT_SUB * qb + u - i
            start = pl.multiple_of(jnp.maximum(j, 0) * ATT_TILE, ATT_TILE)
            kb = k_ref[0, pl.ds(start, ATT_TILE), :]
            z = lax.dot_general(q_heads[u], kb, (((1,), (1,)), ((), ())), preferred_element_type=_F32)
            if diagonal:
                z = jnp.where(causal, z, MASKED_SCORE)
            soft = jnp.maximum(z, 0.0) + jnp.log(1.0 + jnp.exp2(jnp.abs(z) * (-LOG2_E)))
            starts.append((j, start))
            scores.append(z)
            softs.append(soft.astype(_BF16))
        csum_all = _dot(jnp.concatenate(softs, axis=0), suffix_ones)
        least = None
        for u in range(ATT_SUB):
            j, start = starts[u]
            csum = csum_all[u * both:(u + 1) * both, :]
            total = jnp.broadcast_to(csum[:, 0:1], (both, HEAD_PAIR))
            vb = v_ref[0, pl.ds(start, ATT_TILE), :]
            if diagonal:
                w = jnp.exp(scores[u] - csum)
                acc_ref[u] = _dot(w.astype(_BF16), vb)
                rest = total
            else:
                rest = jnp.where(j >= 0, rest_ref[u], -MASKED_SCORE)
                w = jnp.exp(scores[u] - csum - jnp.concatenate([rest] * (ATT_TILE // HEAD_PAIR), axis=1))
                acc_ref[u] += _dot(w.astype(_BF16), vb)
                rest = rest + total
            rest_ref[u] = rest
            least = rest if least is None else jnp.minimum(least, rest)
        return jnp.min(least) < -F32_EXP_ZERO_BELOW

    more = sweep(0, True)
    last = ATT_SUB * qb + ATT_SUB - 1

    def cond(carry):
        i, more = carry
        return jnp.logical_and(i <= last, more)

    def body(carry):
        i, _ = carry
        return i + 1, sweep(i, False)

    lax.while_loop(cond, body, (jnp.int32(1), more))
    for u in range(ATT_SUB):
        out = jnp.where(lane < HEAD_DIM, acc_ref[u, 0:ATT_TILE, :], acc_ref[u, ATT_TILE:both, :])
        o_ref[0, u * ATT_TILE:(u + 1) * ATT_TILE, :] = out.astype(o_ref.dtype)


def _sb_attention(proj, kv):
    bsz, seq, _ = proj.shape
    n_pairs = SB_W // HEAD_PAIR
    return pl.pallas_call(
        _sb_body,
        out_shape=jax.ShapeDtypeStruct((bsz, seq, SB_W), _BF16),
        grid=(bsz, n_pairs, seq // ATT_QBLOCK),
        in_specs=[
            pl.BlockSpec((1, ATT_QBLOCK, HEAD_PAIR), lambda b, p, i: (b, i, p)),
            pl.BlockSpec((1, seq, HEAD_PAIR), lambda b, p, i: (b, 0, p)),
            pl.BlockSpec((1, seq, HEAD_PAIR), lambda b, p, i: (b, 0, n_pairs + p)),
        ],
        out_specs=pl.BlockSpec((1, ATT_QBLOCK, HEAD_PAIR), lambda b, p, i: (b, i, p)),
        scratch_shapes=[pltpu.VMEM((ATT_SUB, 2 * ATT_TILE, HEAD_PAIR), _F32),
                        pltpu.VMEM((ATT_SUB, 2 * ATT_TILE, HEAD_PAIR), _F32)],
        compiler_params=_params(("arbitrary", "arbitrary", "arbitrary")),
        name="sb_attention",
    )(proj, kv, kv)


def kernel(x, mem, ffn_w_gate, ffn_w_up, ffn_w_down, norm_g, mem_norm_g, w_mem_kv, w_mix_out,
           a_w_in, a_conv_w, a_conv_b, a_gate_w, a_gate_b, a_lambda, b_w_in, kv_norm_g,
           w_kv_shared):
    row1 = lambda v: v.reshape(1, -1).astype(_F32)
    bf = lambda w: w.astype(_BF16)

    ffn_w = (bf(ffn_w_gate), bf(ffn_w_up), bf(ffn_w_down))

    def mem_kv(layer):
        return _mem_block_diag(_memkv(mem, row1(mem_norm_g[layer]), bf(w_mem_kv[layer])))

    x, proj = _block(x, ffn_w, (0, 0), row1(norm_g[0, 0]), row1(norm_g[0, 1]),
                     proj=(row1(norm_g[0, 2]), bf(a_w_in[0]), _F32))
    rec = (proj, a_conv_w[0], row1(a_conv_b[0]), _gate_block_diag(a_gate_w[0]), row1(a_gate_b[0]),
           row1(a_lambda[0]))
    kbd, vbd = mem_kv(0)
    x, kv = _block(x, ffn_w, (0, 1), row1(norm_g[0, 4]), row1(norm_g[0, 5]),
                   mix=(rec, proj, (2 * REC_W) // MEM_W, kbd, vbd, bf(w_mix_out[0]), row1(norm_g[0, 3])),
                   proj=(row1(kv_norm_g), bf(w_kv_shared), _BF16))

    x, proj = _block(x, ffn_w, (1, 0), row1(norm_g[1, 0]), row1(norm_g[1, 1]),
                     proj=(row1(norm_g[1, 2]), bf(b_w_in[0]), _BF16))
    y_main = _sb_attention(proj, kv)
    kbd, vbd = mem_kv(1)
    x, _ = _block(x, ffn_w, (1, 1), row1(norm_g[1, 4]), row1(norm_g[1, 5]),
                  mix=(y_main, proj, SB_W // MEM_W, kbd, vbd, bf(w_mix_out[1]), row1(norm_g[1, 3])))
    return x
```

```python
import functools
import math

import jax
import jax.numpy as jnp
from jax import lax
from jax.experimental import pallas as pl
from jax.experimental.pallas import tpu as pltpu

D_MODEL = 1024
HEAD_DIM = 64
MEM_HEADS = 4
MEM_W = MEM_HEADS * HEAD_DIM
REC_W = D_MODEL - MEM_W
SB_W = D_MODEL - MEM_W
N_MEM = 256
CONV_W = 4
LRU_C = 8.0
D_FF = 2816
EPS = 1e-6

SUBLANES = 8
MXU_DIM = 256
VMEM_LIMIT_BYTES = 60 * 1024 * 1024

ROW_TILE = 512
FF_CHUNK = MXU_DIM
N_FF_CHUNKS = D_FF // FF_CHUNK
PREP_AFTER_CHUNK = (1, 5)
PREP_ROWS = ROW_TILE // len(PREP_AFTER_CHUNK)
ATT_TILE = 256
ATT_QBLOCK = 2048
ATT_SUB = ATT_QBLOCK // ATT_TILE
HEAD_PAIR = 2 * HEAD_DIM
MASKED_SCORE = -1e30
LOG2_E = math.log2(math.e)

F32_EXP_ZERO_BELOW = -110.0

_F32 = jnp.float32
_BF16 = jnp.bfloat16


def _params(semantics):
    return pltpu.CompilerParams(dimension_semantics=semantics, vmem_limit_bytes=VMEM_LIMIT_BYTES)


def _resident(shape):
    zeros = (0,) * len(shape)
    return pl.BlockSpec(shape, lambda *_: zeros, pipeline_mode=pl.Buffered(1))


def _rms(x, g):
    ms = jnp.mean(x * x, axis=-1, keepdims=True)
    return x * lax.rsqrt(ms + EPS) * g


def _softplus(z):
    return jnp.maximum(z, 0.0) + jnp.log(1.0 + jnp.exp(-jnp.abs(z)))


def _dot(a, b):
    return jnp.dot(a, b, preferred_element_type=_F32)


def _mem_attention(qm, kbd, vbd):
    scores = _dot(qm, kbd)
    probs = []
    for h in range(MEM_HEADS):
        sh = scores[:, h * N_MEM:(h + 1) * N_MEM]
        e = jnp.exp(sh - jnp.max(sh, axis=-1, keepdims=True))
        probs.append(e / jnp.sum(e, axis=-1, keepdims=True))
    p = jnp.concatenate(probs, axis=-1).astype(_BF16)
    return _dot(p, vbd)


def _block_body(*refs, has_mix, has_rec, has_proj, n_tiles, per_seq):
    refs = list(refs)
    x_ref = refs.pop(0)
    if has_rec:
        xrec_ref, xgate_ref = refs[:2]
        rec_refs = refs[2:7]
        del refs[:7]
    elif has_mix:
        ymain_ref = refs.pop(0)
    if has_mix:
        qmem_ref, kbd_ref, vbd_ref, wout_ref, gmix_ref = refs[:5]
        del refs[:5]
    gin_ref, gout_ref, wg_ref, wu_ref, wd_ref = refs[:5]
    del refs[:5]
    if has_proj:
        gp_ref, wp_ref = refs[:2]
        del refs[:2]
    o_ref = refs.pop(0)
    if has_proj:
        p_ref = refs.pop(0)
    if has_mix:
        xres_refs, xn_refs, acc_ref = refs[0:2], refs[2:4], refs[4]
        del refs[:5]
    else:
        xres_refs, xn_refs, acc_ref = refs[0:1], refs[1:2], refs[2]
        del refs[:3]
    if has_rec:
        xbuf_ref, h_ref = refs

    t = pl.program_id(0)

    def prepare(slot, rows, tile):
        x = x_ref[0, rows, :]
        if has_mix:
            if has_rec:
                first = (tile % per_seq == 0) if rows.start == 0 else False
                ymain = _rglru_rows(xrec_ref[0, rows, :], xgate_ref[0, rows, :], first, *rec_refs,
                                    xbuf_ref, h_ref).astype(_BF16)
            else:
                ymain = ymain_ref[0, rows, :]
            ymem = _mem_attention(qmem_ref[0, rows, :].astype(_BF16), kbd_ref[0], vbd_ref[0]).astype(_BF16)
            mix = _dot(ymain, wout_ref[0:SB_W, :]) + _dot(ymem, wout_ref[SB_W:, :])
            x = x + _rms(mix, gmix_ref[...])
        xres_refs[slot][rows, :] = x
        xn_refs[slot][rows, :] = _rms(x, gin_ref[...]).astype(_BF16)

    def ffn(slot, between=None):
        xn_ref = xn_refs[slot]
        for c in range(N_FF_CHUNKS):
            cols = slice(c * FF_CHUNK, (c + 1) * FF_CHUNK)
            gate = _dot(xn_ref[...], wg_ref[:, cols])
            up = _dot(xn_ref[...], wu_ref[:, cols])
            act = (gate * jax.nn.sigmoid(gate) * up).astype(_BF16)
            down = _dot(act, wd_ref[cols, :])
            if c == 0:
                acc_ref[...] = down
            else:
                acc_ref[...] += down
            if between is not None and c in PREP_AFTER_CHUNK:
                between(PREP_AFTER_CHUNK.index(c))

    def finish(slot):
        xo = xres_refs[slot][...] + 0.5 * _rms(acc_ref[...], gout_ref[...])
        o_ref[0] = xo
        if has_proj:
            p_ref[0] = _dot(_rms(xo, gp_ref[...]).astype(_BF16), wp_ref[...]).astype(p_ref.dtype)

    slab_rows = lambda slab: slice(slab * PREP_ROWS, (slab + 1) * PREP_ROWS)

    if not has_mix:
        prepare(0, slice(0, ROW_TILE), t)
        ffn(0)
        finish(0)
        return

    def first_step():
        if has_rec:
            xbuf_ref[...] = jnp.zeros_like(xbuf_ref)
            h_ref[...] = jnp.zeros_like(h_ref)
        for slab in range(len(PREP_AFTER_CHUNK)):
            prepare(1, slab_rows(slab), t)

    def later_step(prepare_next):
        xn_refs[0][...] = xn_refs[1][...]
        xres_refs[0][...] = xres_refs[1][...]
        ffn(0, between=(lambda slab: prepare(1, slab_rows(slab), t)) if prepare_next else None)
        finish(0)

    pl.when(t == 0)(first_step)
    pl.when((t > 0) & (t < n_tiles))(functools.partial(later_step, True))
    pl.when(t == n_tiles)(functools.partial(later_step, False))


def _block(x, ffn_w, ffn_at, g_in, g_out, mix=None, proj=None):
    bsz, seq, _ = x.shape
    per_seq = seq // ROW_TILE
    n_tiles = bsz * per_seq
    has_mix = mix is not None
    has_rec = has_mix and isinstance(mix[0], tuple)

    if has_mix:
        fed = lambda t: jnp.minimum(t, n_tiles - 1)
        done = lambda t: jnp.maximum(t - 1, 0)
        n_steps = n_tiles + 1
    else:
        fed = done = lambda t: t
        n_steps = n_tiles

    def row_spec(width, tile_of, col=0):
        return pl.BlockSpec((1, ROW_TILE, width),
                            lambda t: (tile_of(t) // per_seq, tile_of(t) % per_seq, col))

    per_batch = lambda a: pl.BlockSpec((1,) + a.shape[1:], lambda t: (fed(t) // per_seq, 0, 0))
    gain = _resident((1, D_MODEL))

    args, specs = [x], [row_spec(D_MODEL, fed)]
    scratch = []
    if has_mix:
        y_src, q_src, q_block, kbd, vbd, w_out, g_mix = mix
        if has_rec:
            rec_in, *rec_params = y_src
            args += [rec_in, rec_in, *rec_params]
            specs += [row_spec(REC_W, fed, 0), row_spec(REC_W, fed, 1)] + [_resident(p.shape) for p in rec_params]
            scratch = [pltpu.VMEM((PREP_ROWS + SUBLANES, REC_W), _F32), pltpu.VMEM((1, REC_W), _F32)]
        else:
            args += [y_src]
            specs += [row_spec(SB_W, fed)]
        args += [q_src, kbd, vbd, w_out, g_mix]
        specs += [row_spec(MEM_W, fed, q_block), per_batch(kbd), per_batch(vbd), _resident(w_out.shape), gain]
    args += [g_in, g_out, *ffn_w]
    specs += [gain, gain] + [
        pl.BlockSpec((None, None) + w.shape[2:], lambda t: (*ffn_at, 0, 0), pipeline_mode=pl.Buffered(1))
        for w in ffn_w]
    out_shape = [jax.ShapeDtypeStruct((bsz, seq, D_MODEL), _F32)]
    out_specs = [row_spec(D_MODEL, done)]
    if proj is not None:
        g_proj, w_proj, p_dtype = proj
        args += [g_proj, w_proj]
        specs += [gain, _resident(w_proj.shape)]
        out_shape.append(jax.ShapeDtypeStruct((bsz, seq, w_proj.shape[1]), p_dtype))
        out_specs.append(row_spec(w_proj.shape[1], done))

    slots = 2 if has_mix else 1
    outs = pl.pallas_call(
        functools.partial(_block_body, has_mix=has_mix, has_rec=has_rec, has_proj=proj is not None,
                          n_tiles=n_tiles, per_seq=per_seq),
        out_shape=out_shape,
        grid=(n_steps,),
        in_specs=specs,
        out_specs=out_specs,
        scratch_shapes=([pltpu.VMEM((ROW_TILE, D_MODEL), _F32)] * slots
                        + [pltpu.VMEM((ROW_TILE, D_MODEL), _BF16)] * slots
                        + [pltpu.VMEM((ROW_TILE, D_MODEL), _F32)] + scratch),
        compiler_params=_params(("arbitrary",)),
        name="block" + ("_rec" if has_rec else "") + ("_mix" if has_mix else "") + ("_proj" if proj is not None else ""),
    )(*args)
    return outs if proj is not None else (outs[0], None)


def _memkv_body(mem_ref, g_ref, w_ref, o_ref):
    mn = _rms(mem_ref[0], g_ref[...]).astype(_BF16)
    o_ref[0] = _dot(mn, w_ref[...])


def _memkv(mem, g, w):
    bsz = mem.shape[0]
    return pl.pallas_call(
        _memkv_body,
        out_shape=jax.ShapeDtypeStruct((bsz, N_MEM, 2 * MEM_W), _F32),
        grid=(bsz,),
        in_specs=[
            pl.BlockSpec((1, N_MEM, D_MODEL), lambda b: (b, 0, 0)),
            pl.BlockSpec((1, D_MODEL), lambda b: (0, 0)),
            pl.BlockSpec((D_MODEL, 2 * MEM_W), lambda b: (0, 0)),
        ],
        out_specs=pl.BlockSpec((1, N_MEM, 2 * MEM_W), lambda b: (b, 0, 0)),
        compiler_params=_params(("arbitrary",)),
        name="memkv",
    )(mem, g, w)


def _mem_block_diag(kvm):
    bsz = kvm.shape[0]
    k = kvm[:, :, :MEM_W].reshape(bsz, N_MEM, MEM_HEADS, HEAD_DIM)
    v = kvm[:, :, MEM_W:].reshape(bsz, N_MEM, MEM_HEADS, HEAD_DIM)
    eye = jnp.eye(MEM_HEADS, dtype=_F32)
    kbd = jnp.einsum("bmhd,hg->bhdgm", k * (1.0 / math.sqrt(HEAD_DIM)), eye)
    kbd = kbd.reshape(bsz, MEM_W, MEM_HEADS * N_MEM)
    vbd = jnp.einsum("bmhd,hg->bgmhd", v, eye).reshape(bsz, MEM_HEADS * N_MEM, MEM_W)
    return kbd.astype(_BF16), vbd.astype(_BF16)


def _rglru_rows(xr, xg, first, cw_ref, cb_ref, wg_ref, gb_ref, lam_ref, xbuf_ref, h_ref):
    rows = xr.shape[0]
    tail = xbuf_ref[rows:rows + SUBLANES, :]
    xbuf_ref[0:SUBLANES, :] = jnp.where(first, 0.0, tail)
    xbuf_ref[SUBLANES:, :] = xr
    xc = cb_ref[...] + cw_ref[CONV_W - 1:CONV_W, :] * xr
    for k in range(CONV_W - 1):
        off = SUBLANES - (CONV_W - 1) + k
        xc = xc + cw_ref[k:k + 1, :] * xbuf_ref[off:off + rows, :]

    gates = _dot(xc.astype(_BF16), wg_ref[...]) + gb_ref[...]
    r = jax.nn.sigmoid(gates[:, :REC_W])
    i = jax.nn.sigmoid(gates[:, REC_W:])
    log_a = (-LRU_C) * r * _softplus(-lam_ref[...])
    a = jnp.exp(log_a)
    var = -jnp.tanh(log_a) * (1.0 + a * a)
    b = jnp.where(var > 0.0, var * lax.rsqrt(var), 0.0) * (i * xc)

    n_groups = rows // SUBLANES
    a3 = a.reshape(n_groups, SUBLANES, REC_W)
    b3 = b.reshape(n_groups, SUBLANES, REC_W)
    sub = lax.broadcasted_iota(jnp.int32, (n_groups, SUBLANES, REC_W), 1)
    d = 1
    while d < SUBLANES:
        keep = sub >= d
        a_prev = jnp.where(keep, pltpu.roll(a3, d, 1), 1.0)
        b_prev = jnp.where(keep, pltpu.roll(b3, d, 1), 0.0)
        b3 = a3 * b_prev + b3
        a3 = a3 * a_prev
        d *= 2
    h = jnp.where(first, 0.0, h_ref[...])
    outs = []
    for g in range(n_groups):
        hg = a3[g] * h + b3[g]
        outs.append(hg)
        h = hg[SUBLANES - 1:SUBLANES, :]
    h_ref[...] = h
    h_all = jnp.concatenate(outs, axis=0)
    return jax.nn.gelu(xg, approximate=True) * h_all


def _gate_block_diag(gate_w):
    n_blocks = gate_w.shape[1]
    eye = jnp.eye(n_blocks, dtype=gate_w.dtype)
    bd = jnp.einsum("gncd,nm->gncmd", gate_w, eye).reshape(2, REC_W, REC_W)
    return jnp.concatenate([bd[0], bd[1]], axis=1).astype(_BF16)


def _sb_body(q_ref, k_ref, v_ref, o_ref, acc_ref, rest_ref):
    qb = pl.program_id(2)
    both = 2 * ATT_TILE
    lane = lax.broadcasted_iota(jnp.int32, (ATT_TILE, HEAD_PAIR), 1)
    row = lax.broadcasted_iota(jnp.int32, (ATT_TILE, ATT_TILE), 0)
    col = lax.broadcasted_iota(jnp.int32, (ATT_TILE, ATT_TILE), 1)
    suffix_ones = (row >= col).astype(_BF16)
    row2 = lax.broadcasted_iota(jnp.int32, (both, ATT_TILE), 0)
    col2 = lax.broadcasted_iota(jnp.int32, (both, ATT_TILE), 1)
    causal = col2 < jnp.where(row2 >= ATT_TILE, row2 - ATT_TILE, row2)
    scale = jnp.asarray(1.0 / math.sqrt(HEAD_DIM), _BF16)

    q_heads = []
    for u in range(ATT_SUB):
        q = q_ref[0, u * ATT_TILE:(u + 1) * ATT_TILE, :] * scale
        zero = jnp.zeros_like(q)
        q_heads.append(jnp.concatenate([jnp.where(lane < HEAD_DIM, q, zero),
                                        jnp.where(lane < HEAD_DIM, zero, q)], axis=0))

    def sweep(i, diagonal):
        starts, scores, softs = [], [], []
        for u in range(ATT_SUB):
            j = ATT_SUB * qb + u - i
            start = pl.multiple_of(jnp.maximum(j, 0) * ATT_TILE, ATT_TILE)
            kb = k_ref[0, pl.ds(start, ATT_TILE), :]
            z = lax.dot_general(q_heads[u], kb, (((1,), (1,)), ((), ())), preferred_element_type=_F32)
            if diagonal:
                z = jnp.where(causal, z, MASKED_SCORE)
            soft = jnp.maximum(z, 0.0) + jnp.log(1.0 + jnp.exp2(jnp.abs(z) * (-LOG2_E)))
            starts.append((j, start))
            scores.append(z)
            softs.append(soft.astype(_BF16))
        csum_all = _dot(jnp.concatenate(softs, axis=0), suffix_ones)
        least = None
        for u in range(ATT_SUB):
            j, start = starts[u]
            csum = csum_all[u * both:(u + 1) * both, :]
            total = jnp.broadcast_to(csum[:, 0:1], (both, HEAD_PAIR))
            vb = v_ref[0, pl.ds(start, ATT_TILE), :]
            if diagonal:
                w = jnp.exp(scores[u] - csum)
                acc_ref[u] = _dot(w.astype(_BF16), vb)
                rest = total
            else:
                rest = jnp.where(j >= 0, rest_ref[u], -MASKED_SCORE)
                w = jnp.exp(scores[u] - csum - jnp.concatenate([rest] * (ATT_TILE // HEAD_PAIR), axis=1))
                acc_ref[u] += _dot(w.astype(_BF16), vb)
                rest = rest + total
            rest_ref[u] = rest
            least = rest if least is None else jnp.minimum(least, rest)
        return jnp.min(least) < -F32_EXP_ZERO_BELOW

    more = sweep(0, True)
    last = ATT_SUB * qb + ATT_SUB - 1

    def cond(carry):
        i, more = carry
        return i <= last

    def body(carry):
        i, _ = carry
        return i + 1, sweep(i, False)

    lax.while_loop(cond, body, (jnp.int32(1), more))
    for u in range(ATT_SUB):
        out = jnp.where(lane < HEAD_DIM, acc_ref[u, 0:ATT_TILE, :], acc_ref[u, ATT_TILE:both, :])
        o_ref[0, u * ATT_TILE:(u + 1) * ATT_TILE, :] = out.astype(o_ref.dtype)


def _sb_attention(proj, kv):
    bsz, seq, _ = proj.shape
    n_pairs = SB_W // HEAD_PAIR
    return pl.pallas_call(
        _sb_body,
        out_shape=jax.ShapeDtypeStruct((bsz, seq, SB_W), _BF16),
        grid=(bsz, n_pairs, seq // ATT_QBLOCK),
        in_specs=[
            pl.BlockSpec((1, ATT_QBLOCK, HEAD_PAIR), lambda b, p, i: (b, i, p)),
            pl.BlockSpec((1, seq, HEAD_PAIR), lambda b, p, i: (b, 0, p)),
            pl.BlockSpec((1, seq, HEAD_PAIR), lambda b, p, i: (b, 0, n_pairs + p)),
        ],
        out_specs=pl.BlockSpec((1, ATT_QBLOCK, HEAD_PAIR), lambda b, p, i: (b, i, p)),
        scratch_shapes=[pltpu.VMEM((ATT_SUB, 2 * ATT_TILE, HEAD_PAIR), _F32),
                        pltpu.VMEM((ATT_SUB, 2 * ATT_TILE, HEAD_PAIR), _F32)],
        compiler_params=_params(("arbitrary", "arbitrary", "arbitrary")),
        name="sb_attention",
    )(proj, kv, kv)


def kernel(x, mem, ffn_w_gate, ffn_w_up, ffn_w_down, norm_g, mem_norm_g, w_mem_kv, w_mix_out,
           a_w_in, a_conv_w, a_conv_b, a_gate_w, a_gate_b, a_lambda, b_w_in, kv_norm_g,
           w_kv_shared):
    row1 = lambda v: v.reshape(1, -1).astype(_F32)
    bf = lambda w: w.astype(_BF16)

    ffn_w = (bf(ffn_w_gate), bf(ffn_w_up), bf(ffn_w_down))

    def mem_kv(layer):
        return _mem_block_diag(_memkv(mem, row1(mem_norm_g[layer]), bf(w_mem_kv[layer])))

    x, proj = _block(x, ffn_w, (0, 0), row1(norm_g[0, 0]), row1(norm_g[0, 1]),
                     proj=(row1(norm_g[0, 2]), bf(a_w_in[0]), _F32))
    rec = (proj, a_conv_w[0], row1(a_conv_b[0]), _gate_block_diag(a_gate_w[0]), row1(a_gate_b[0]),
           row1(a_lambda[0]))
    kbd, vbd = mem_kv(0)
    x, kv = _block(x, ffn_w, (0, 1), row1(norm_g[0, 4]), row1(norm_g[0, 5]),
                   mix=(rec, proj, (2 * REC_W) // MEM_W, kbd, vbd, bf(w_mix_out[0]), row1(norm_g[0, 3])),
                   proj=(row1(kv_norm_g), bf(w_kv_shared), _BF16))

    x, proj = _block(x, ffn_w, (1, 0), row1(norm_g[1, 0]), row1(norm_g[1, 1]),
                     proj=(row1(norm_g[1, 2]), bf(b_w_in[0]), _BF16))
    y_main = _sb_attention(proj, kv)
    kbd, vbd = mem_kv(1)
    x, _ = _block(x, ffn_w, (1, 1), row1(norm_g[1, 4]), row1(norm_g[1, 5]),
                  mix=(y_main, proj, SB_W // MEM_W, kbd, vbd, bf(w_mix_out[1]), row1(norm_g[1, 3])))
    return x
```
